```python
import math
import jax, jax.numpy as jnp
from jax import lax
import numpy as np

D_MODEL = 2048
BATCH = 16
SEQ = 256
DEPTH = 4
DEC_BATCH = 8
DEC_SEQ = 2048
PAST_LEN = 512

GRID_W = 64
N_MIXERS = 3
Q_BLOCK = 128
ROPE_THETA = 10000.0
EPS = 1e-6
N_MOD = 6
CONV_W = 31
CONV_PAD = CONV_W // 2
N_HEADS = 16
N_KV_HEADS = 4
HEAD_DIM = D_MODEL // N_HEADS
GQA_GROUP = N_HEADS // N_KV_HEADS
MLA_HEADS = 16
Q_RANK = 512
KV_RANK = 512
NOPE_DIM = 128
ROPE_DIM = 64
V_DIM = 128
D_FF = 5632
N_EXPERTS = 8
TOP_K = 2
D_FF_EXPERT = 2816
N_CONV = (DEPTH + 2) // 3
N_GQA = (DEPTH + 1) // 3
N_MLA = DEPTH // 3
N_DENSE = (DEPTH + 1) // 2
N_MOE = DEPTH // 2

kernel_name = "hybrid_diffusion_conv_gqa_mla_moe_step"


def rms_norm(x, g):
    xf = x.astype(jnp.float32)
    y = xf * lax.rsqrt(jnp.mean(xf * xf, axis=-1, keepdims=True) + EPS)
    return (y * g.astype(jnp.float32)).astype(x.dtype)


def layer_norm(x, g, b):
    xf = x.astype(jnp.float32)
    mu = jnp.mean(xf, axis=-1, keepdims=True)
    xc = xf - mu
    var = jnp.mean(xc * xc, axis=-1, keepdims=True)
    return (xc * lax.rsqrt(var + EPS) * g.astype(jnp.float32) + b.astype(jnp.float32)).astype(x.dtype)


def rope_tables(n_tokens, dim, dtype):
    rows = n_tokens // GRID_W
    row = jnp.repeat(jnp.arange(rows, dtype=jnp.float32), GRID_W)
    col = jnp.tile(jnp.arange(GRID_W, dtype=jnp.float32), rows)
    half = dim // 2
    inv = ROPE_THETA ** (-jnp.arange(0, half, 2, dtype=jnp.float32) / half)
    ar = row[:, None] * inv[None, :]
    ac = col[:, None] * inv[None, :]
    ang = jnp.concatenate([ar, ar, ac, ac], axis=-1)
    return jnp.cos(ang).astype(dtype)[:, None, :], jnp.sin(ang).astype(dtype)[:, None, :]


def _rotate_half(x):
    a, b = jnp.split(x, 2, axis=-1)
    return jnp.concatenate([-b, a], axis=-1)


def apply_rope_2d(x, rope):
    cos, sin = rope
    half = x.shape[-1] // 2
    rot = jnp.concatenate([_rotate_half(x[..., :half]), _rotate_half(x[..., half:])], axis=-1)
    return x * cos + rot * sin


def attend(q, k, v, scale):
    b, t = q.shape[0], q.shape[1]
    nb = t // Q_BLOCK
    qb = jnp.moveaxis(q.reshape((b, nb, Q_BLOCK) + q.shape[2:]), 1, 0)

    def block(qi):
        s = jnp.einsum('bqkgd,bskd->bkgqs', qi, k, preferred_element_type=jnp.float32) * scale
        p = jax.nn.softmax(s, axis=-1).astype(v.dtype)
        return jnp.einsum('bkgqs,bske->bqkge', p, v)

    o = lax.map(block, qb)
    return jnp.moveaxis(o, 0, 1).reshape((b, t) + o.shape[3:])


def conv_module(h, w1, b1, wdw, bdw, lng, lnb, w2, b2):
    u = h @ w1 + b1
    a, gt = jnp.split(u, 2, axis=-1)
    u = a * jax.nn.sigmoid(gt)
    u = lax.conv_general_dilated(u, wdw[:, None, :], window_strides=(1,),
                                 padding=[(CONV_PAD, CONV_PAD)],
                                 dimension_numbers=('NWC', 'WIO', 'NWC'),
                                 feature_group_count=u.shape[-1]) + bdw
    u = jax.nn.silu(layer_norm(u, lng, lnb))
    return u @ w2 + b2


def gqa_mixer(h, wq, wk, wv, gq, gk, wo, rope, ctx_kv):
    b, t, _ = h.shape
    q = rms_norm((h @ wq).reshape(b, t, N_HEADS, HEAD_DIM), gq)
    k = rms_norm((h @ wk).reshape(b, t, N_KV_HEADS, HEAD_DIM), gk)
    v = (h @ wv).reshape(b, t, N_KV_HEADS, HEAD_DIM)
    if ctx_kv is None:
        keys, vals = k, v
    else:
        q = apply_rope_2d(q, rope)
        k = apply_rope_2d(k, rope)
        keys = jnp.concatenate([ctx_kv[0], k], axis=1)
        vals = jnp.concatenate([ctx_kv[1], v], axis=1)
    o = attend(q.reshape(b, t, N_KV_HEADS, GQA_GROUP, HEAD_DIM), keys, vals, HEAD_DIM ** -0.5)
    return o.reshape(b, t, N_HEADS * HEAD_DIM) @ wo, k, v


def mla_mixer(h, wdq, gq, wuq, wdkv, gkv, wuk, wuv, wo, rope, ctx_lat):
    b, t, _ = h.shape
    cq = rms_norm(h @ wdq, gq)
    q = jnp.einsum('btr,rhd->bthd', cq, wuq)
    q_nope, q_pe = q[..., :NOPE_DIM], q[..., NOPE_DIM:]
    kv = h @ wdkv
    ckv = rms_norm(kv[..., :KV_RANK], gkv)
    kpe = kv[..., KV_RANK:]
    if ctx_lat is None:
        keys_c, keys_pe = ckv, kpe
    else:
        q_pe = apply_rope_2d(q_pe, rope)
        kpe_r = apply_rope_2d(kpe[:, :, None, :], rope)[:, :, 0, :]
        keys_c = jnp.concatenate([ctx_lat[0], ckv], axis=1)
        keys_pe = jnp.concatenate([ctx_lat[1], kpe_r], axis=1)
    q_lat = jnp.einsum('bthn,rhn->bthr', q_nope, wuk)
    qa = jnp.concatenate([q_lat, q_pe], axis=-1)[:, :, None]
    ka = jnp.concatenate([keys_c, keys_pe], axis=-1)[:, :, None]
    o = attend(qa, ka, keys_c[:, :, None], (NOPE_DIM + ROPE_DIM) ** -0.5)[:, :, 0]
    o = jnp.einsum('bthr,rhv->bthv', o, wuv).reshape(b, t, MLA_HEADS * V_DIM)
    return o @ wo, ckv, kpe


def swiglu(h, wg, wu, wd):
    return (jax.nn.silu(h @ wg) * (h @ wu)) @ wd


def moe_swiglu(h, wr, br, wg, wu, wd):
    b, t, d = h.shape
    hf = h.reshape(b * t, d)
    logits = (hf @ wr + br).astype(jnp.float32)
    top_v, top_i = lax.top_k(logits, TOP_K)
    w = jax.nn.softmax(top_v, axis=-1)
    gate = jnp.sum(jax.nn.one_hot(top_i, N_EXPERTS, dtype=jnp.float32) * w[..., None], axis=1)
    gate = gate.astype(h.dtype)
    out = jnp.zeros_like(hf)
    for e in range(N_EXPERTS):
        out = out + gate[:, e:e + 1] * swiglu(hf, wg[e], wu[e], wd[e])
    return out.reshape(b, t, d)


def trunk(x, cond, P, rope_hd, rope_pe, cache):
    ks, vs, cs, ps = [], [], [], []
    for i in range(DEPTH):
        mod = (jax.nn.silu(cond) @ P['w_mod'][i] + P['b_mod'][i])[:, None, :]
        sh1, sc1, g1, sh2, sc2, g2 = jnp.split(mod, N_MOD, axis=-1)
        ng = P['norm_g'][i]
        h = rms_norm(x, ng[0]) * (1 + sc1) + sh1
        j = i // N_MIXERS
        if i % N_MIXERS == 0:
            out = conv_module(h, P['conv_w1'][j], P['conv_b1'][j], P['conv_wdw'][j], P['conv_bdw'][j],
                              P['conv_ln_g'][j], P['conv_ln_b'][j], P['conv_w2'][j], P['conv_b2'][j])
        elif i % N_MIXERS == 1:
            ctx = None if cache is None else (cache[0][:, j], cache[1][:, j])
            out, k, v = gqa_mixer(h, P['gqa_wq'][j], P['gqa_wk'][j], P['gqa_wv'][j], P['gqa_gq'][j],
                                  P['gqa_gk'][j], P['gqa_wo'][j], rope_hd, ctx)
            ks.append(k)
            vs.append(v)
        else:
            ctx = None if cache is None else (cache[2][:, j], cache[3][:, j])
            out, ckv, kpe = mla_mixer(h, P['mla_wdq'][j], P['mla_gq'][j], P['mla_wuq'][j], P['mla_wdkv'][j],
                                      P['mla_gkv'][j], P['mla_wuk'][j], P['mla_wuv'][j], P['mla_wo'][j],
                                      rope_pe, ctx)
            cs.append(ckv)
            ps.append(kpe)
        x = x + g1 * rms_norm(out, ng[1])
        h = rms_norm(x, ng[2]) * (1 + sc2) + sh2
        f = i // 2
        if i % 2 == 0:
            out = swiglu(h, P['ffn_wg'][f], P['ffn_wu'][f], P['ffn_wd'][f])
        else:
            out = moe_swiglu(h, P['moe_wr'][f], P['moe_br'][f], P['moe_wg'][f], P['moe_wu'][f], P['moe_wd'][f])
        x = x + g2 * rms_norm(out, ng[3])
    return x, ks, vs, cs, ps


def setup_inputs(seed: int = 0) -> dict:
    key = jax.random.key(seed)
    keys = iter(jax.random.split(key, 64))
    D = D_MODEL

    def nrm(shape, scale):
        return jax.random.normal(next(keys), shape, jnp.float32) * scale

    def gain(shape):
        return 1.0 + nrm(shape, 0.02)

    inp = {}
    inp['x_prompt'] = nrm((BATCH, SEQ, D), 1.0)
    inp['x_sample'] = nrm((DEC_BATCH, DEC_SEQ, D), 1.0)
    inp['cache_gqa_k'] = nrm((DEC_BATCH, N_GQA, PAST_LEN, N_KV_HEADS, HEAD_DIM), 1.0)
    inp['cache_gqa_v'] = nrm((DEC_BATCH, N_GQA, PAST_LEN, N_KV_HEADS, HEAD_DIM), 1.0)
    inp['cache_mla_ckv'] = nrm((DEC_BATCH, N_MLA, PAST_LEN, KV_RANK), 1.0)
    inp['cache_mla_kpe'] = nrm((DEC_BATCH, N_MLA, PAST_LEN, ROPE_DIM), 1.0)
    inp['c'] = nrm((DEC_BATCH, D), 1.0)
    inp['c_ctx'] = nrm((D,), 1.0)
    inp['w_mod'] = nrm((DEPTH, D, N_MOD * D), 0.5 * D ** -0.5)
    inp['b_mod'] = nrm((DEPTH, N_MOD * D), 0.02)
    inp['norm_g'] = gain((DEPTH, 4, D))
    inp['conv_w1'] = nrm((N_CONV, D, 2 * D), D ** -0.5)
    inp['conv_b1'] = nrm((N_CONV, 2 * D), 0.02)
    inp['conv_wdw'] = nrm((N_CONV, CONV_W, D), CONV_W ** -0.5)
    inp['conv_bdw'] = nrm((N_CONV, D), 0.02)
    inp['conv_ln_g'] = gain((N_CONV, D))
    inp['conv_ln_b'] = nrm((N_CONV, D), 0.02)
    inp['conv_w2'] = nrm((N_CONV, D, D), D ** -0.5)
    inp['conv_b2'] = nrm((N_CONV, D), 0.02)
    inp['gqa_wq'] = nrm((N_GQA, D, N_HEADS * HEAD_DIM), D ** -0.5)
    inp['gqa_wk'] = nrm((N_GQA, D, N_KV_HEADS * HEAD_DIM), D ** -0.5)
    inp['gqa_wv'] = nrm((N_GQA, D, N_KV_HEADS * HEAD_DIM), D ** -0.5)
    inp['gqa_gq'] = gain((N_GQA, HEAD_DIM))
    inp['gqa_gk'] = gain((N_GQA, HEAD_DIM))
    inp['gqa_wo'] = nrm((N_GQA, N_HEADS * HEAD_DIM, D), (N_HEADS * HEAD_DIM) ** -0.5)
    inp['mla_wdq'] = nrm((N_MLA, D, Q_RANK), D ** -0.5)
    inp['mla_gq'] = gain((N_MLA, Q_RANK))
    inp['mla_wuq'] = nrm((N_MLA, Q_RANK, MLA_HEADS, NOPE_DIM + ROPE_DIM), Q_RANK ** -0.5)
    inp['mla_wdkv'] = nrm((N_MLA, D, KV_RANK + ROPE_DIM), D ** -0.5)
    inp['mla_gkv'] = gain((N_MLA, KV_RANK))
    inp['mla_wuk'] = nrm((N_MLA, KV_RANK, MLA_HEADS, NOPE_DIM), KV_RANK ** -0.5)
    inp['mla_wuv'] = nrm((N_MLA, KV_RANK, MLA_HEADS, V_DIM), KV_RANK ** -0.5)
    inp['mla_wo'] = nrm((N_MLA, MLA_HEADS * V_DIM, D), (MLA_HEADS * V_DIM) ** -0.5)
    inp['ffn_wg'] = nrm((N_DENSE, D, D_FF), D ** -0.5)
    inp['ffn_wu'] = nrm((N_DENSE, D, D_FF), D ** -0.5)
    inp['ffn_wd'] = nrm((N_DENSE, D_FF, D), D_FF ** -0.5)
    inp['moe_wr'] = nrm((N_MOE, D, N_EXPERTS), D ** -0.5)
    inp['moe_br'] = nrm((N_MOE, N_EXPERTS), 0.01)
    inp['moe_wg'] = nrm((N_MOE, N_EXPERTS, D, D_FF_EXPERT), D ** -0.5)
    inp['moe_wu'] = nrm((N_MOE, N_EXPERTS, D, D_FF_EXPERT), D ** -0.5)
    inp['moe_wd'] = nrm((N_MOE, N_EXPERTS, D_FF_EXPERT, D), D_FF_EXPERT ** -0.5)
    return inp


def reference(x_prompt, x_sample, cache_gqa_k, cache_gqa_v, cache_mla_ckv, cache_mla_kpe, c, c_ctx,
              w_mod, b_mod, norm_g,
              conv_w1, conv_b1, conv_wdw, conv_bdw, conv_ln_g, conv_ln_b, conv_w2, conv_b2,
              gqa_wq, gqa_wk, gqa_wv, gqa_gq, gqa_gk, gqa_wo,
              mla_wdq, mla_gq, mla_wuq, mla_wdkv, mla_gkv, mla_wuk, mla_wuv, mla_wo,
              ffn_wg, ffn_wu, ffn_wd,
              moe_wr, moe_br, moe_wg, moe_wu, moe_wd):
    P = {
        'w_mod': w_mod, 'b_mod': b_mod, 'norm_g': norm_g,
        'conv_w1': conv_w1, 'conv_b1': conv_b1, 'conv_wdw': conv_wdw, 'conv_bdw': conv_bdw,
        'conv_ln_g': conv_ln_g, 'conv_ln_b': conv_ln_b, 'conv_w2': conv_w2, 'conv_b2': conv_b2,
        'gqa_wq': gqa_wq, 'gqa_wk': gqa_wk, 'gqa_wv': gqa_wv, 'gqa_gq': gqa_gq, 'gqa_gk': gqa_gk,
        'gqa_wo': gqa_wo,
        'mla_wdq': mla_wdq, 'mla_gq': mla_gq, 'mla_wuq': mla_wuq, 'mla_wdkv': mla_wdkv,
        'mla_gkv': mla_gkv, 'mla_wuk': mla_wuk, 'mla_wuv': mla_wuv, 'mla_wo': mla_wo,
        'ffn_wg': ffn_wg, 'ffn_wu': ffn_wu, 'ffn_wd': ffn_wd,
        'moe_wr': moe_wr, 'moe_br': moe_br, 'moe_wg': moe_wg, 'moe_wu': moe_wu, 'moe_wd': moe_wd,
    }
    y_prompt, ks, vs, cs, ps = trunk(x_prompt, c_ctx[None, :], P, None, None, None)
    new_gqa_k = jnp.stack(ks, axis=1)
    new_gqa_v = jnp.stack(vs, axis=1)
    new_mla_ckv = jnp.stack(cs, axis=1)
    new_mla_kpe = jnp.stack(ps, axis=1)
    n_lat = x_sample.shape[1]
    rope_hd = rope_tables(n_lat, HEAD_DIM, x_sample.dtype)
    rope_pe = rope_tables(n_lat, ROPE_DIM, x_sample.dtype)
    y_sample = trunk(x_sample, c, P, rope_hd, rope_pe,
                     (cache_gqa_k, cache_gqa_v, cache_mla_ckv, cache_mla_kpe))[0]
    return (y_prompt, y_sample, new_gqa_k, new_gqa_v, new_mla_ckv, new_mla_kpe)
```

```python
import functools

import jax
import jax.numpy as jnp
from jax import lax
from jax.experimental import pallas as pl
from jax.experimental.pallas import tpu as pltpu

F32 = jnp.float32
BF16 = jnp.bfloat16

EPS = 1e-6
GRID_W = 64
ROPE_THETA = 10000.0
N_MIXERS = 3
N_MOD = 6
CONV_W = 31
CONV_PAD = CONV_W // 2
N_HEADS = 16
N_KV_HEADS = 4
MLA_HEADS = 16
NOPE_DIM = 128
ROPE_DIM = 64
V_DIM = 128
N_EXPERTS = 8

LANES = 128
HALO = 16
MLA_QK = 256
VMEM_LIMIT = 56 * 1024 * 1024

SH1, SC1, G1, SH2, SC2, G2 = range(6)
NG_PRE_MIX, NG_POST_MIX, NG_PRE_FFN, NG_POST_FFN = range(4)


def _cparams(sem):
    return pltpu.CompilerParams(dimension_semantics=sem, vmem_limit_bytes=VMEM_LIMIT)


def _rms(x):
    return x * lax.rsqrt(jnp.mean(x * x, axis=-1, keepdims=True) + EPS)


def _normmod(x, g, sc, sh):
    return (_rms(x) * g) * (1.0 + sc) + sh


def _silu(x):
    return x * jax.nn.sigmoid(x)


def _dot(a, b):
    return jnp.dot(a, b, preferred_element_type=F32)


def _rope(x, cos, sin_signed, hb):
    lane = lax.broadcasted_iota(jnp.int32, x.shape, 1)
    first = (lane % (2 * hb)) < hb
    rot = jnp.where(first, pltpu.roll(x, LANES - hb, 1), pltpu.roll(x, hb, 1))
    return x * cos + rot * sin_signed


class _Geom:
    def __init__(self, b, s, db, t):
        self.B, self.S, self.DB, self.T = b, s, db, t
        self.P = b * s
        self.M = self.P + db * t

    def check_tile(self, tm):
        assert self.P % tm == 0 and self.T % tm == 0, (self.P, self.T, tm)

    def mod_row(self, i, tm):
        npt = self.P // tm
        return jnp.where(i < npt, 0, 1 + (i - npt) // (self.T // tm))

    def rope_blk(self, i, tm):
        npt = self.P // tm
        return jnp.where(i < npt, self.T // tm, (i - npt) % (self.T // tm))


def _mod_kernel(c_ref, w_ref, b_ref, o_ref):
    c = c_ref[...]
    o_ref[...] = _dot(_silu(c).astype(BF16), w_ref[...].astype(BF16)) + b_ref[...]


def _mod_call(cond, w_mod, b_mod):
    depth, d, n = w_mod.shape
    rows = cond.shape[0]
    tn = 1024
    return pl.pallas_call(
        _mod_kernel,
        grid=(depth, n // tn),
        in_specs=[
            pl.BlockSpec((rows, d), lambda l, j: (0, 0)),
            pl.BlockSpec((None, d, tn), lambda l, j: (l, 0, j)),
            pl.BlockSpec((None, 1, tn), lambda l, j: (l, 0, j)),
        ],
        out_specs=pl.BlockSpec((None, rows, tn), lambda l, j: (l, 0, j)),
        out_shape=jax.ShapeDtypeStruct((depth, rows, n), F32),
        compiler_params=_cparams(("arbitrary", "arbitrary")),
        name="mod",
    )(cond, w_mod, b_mod.reshape(depth, 1, n))


def _conv_in_kernel(x_ref, mod_ref, ng_ref, wa_ref, wg_ref, ba_ref, bg_ref, u_ref, h_scr):
    @pl.when(pl.program_id(1) == 0)
    def _():
        h = _normmod(x_ref[...], ng_ref[NG_PRE_MIX:NG_PRE_MIX + 1, :],
                     mod_ref[SC1:SC1 + 1, :], mod_ref[SH1:SH1 + 1, :])
        h_scr[...] = h.astype(BF16)

    h = h_scr[...]
    a = _dot(h, wa_ref[...]) + ba_ref[...]
    g = _dot(h, wg_ref[...]) + bg_ref[...]
    u_ref[...] = a * jax.nn.sigmoid(g)


def _conv_in_call(geo, x, mod, ng, w1, b1):
    m, d = x.shape
    tm, tn = 512, 1024
    geo.check_tile(tm)
    nj = d // tn
    b1 = b1.reshape(1, 2 * d)
    return pl.pallas_call(
        _conv_in_kernel,
        grid=(m // tm, nj),
        in_specs=[
            pl.BlockSpec((tm, d), lambda i, j: (i, 0)),
            pl.BlockSpec((None, N_MOD, d), lambda i, j: (geo.mod_row(i, tm), 0, 0)),
            pl.BlockSpec((4, d), lambda i, j: (0, 0)),
            pl.BlockSpec((d, tn), lambda i, j: (0, j)),
            pl.BlockSpec((d, tn), lambda i, j: (0, j + nj)),
            pl.BlockSpec((1, tn), lambda i, j: (0, j)),
            pl.BlockSpec((1, tn), lambda i, j: (0, j + nj)),
        ],
        out_specs=pl.BlockSpec((tm, tn), lambda i, j: (i, j)),
        out_shape=jax.ShapeDtypeStruct((m, d), F32),
        scratch_shapes=[pltpu.VMEM((tm, d), BF16)],
        compiler_params=_cparams(("arbitrary", "arbitrary")),
        name="conv_in",
    )(x, mod, ng, w1, w1, b1, b1)


CONV_TM = 256
CONV_RC = 32
CONV_CC = 256


def _conv_out_kernel(tiles_p, tps_p, tps_s,
                     u_ref, up_ref, un_ref, x_ref, mod_ref, ng_ref, wdw_ref, bdw_ref,
                     lng_ref, lnb_ref, w2_ref, b2_ref, o_ref, ext, cv):
    i = pl.program_id(0)
    is_p = i < tiles_p
    j = jnp.where(is_p, i % tps_p, (i - tiles_p) % tps_s)
    n = jnp.where(is_p, tps_p, tps_s)
    tm, d = u_ref.shape
    ext[HALO:HALO + tm, :] = u_ref[...]
    ext[0:HALO, :] = jnp.where(j == 0, 0.0, up_ref[...])
    ext[HALO + tm:HALO + tm + HALO, :] = jnp.where(j == n - 1, 0.0, un_ref[...])

    def chunk(c, carry):
        c0 = pl.multiple_of(c * CONV_CC, CONV_CC)
        for r in range(0, tm, CONV_RC):
            acc = jnp.zeros((CONV_RC, CONV_CC), F32)
            for k in range(CONV_W):
                r0 = HALO - CONV_PAD + r + k
                acc = acc + ext[r0:r0 + CONV_RC, pl.ds(c0, CONV_CC)] * wdw_ref[k:k + 1, pl.ds(c0, CONV_CC)]
            cv[r:r + CONV_RC, pl.ds(c0, CONV_CC)] = acc
        return carry

    lax.fori_loop(0, d // CONV_CC, chunk, 0)

    v = cv[...] + bdw_ref[...]
    mu = jnp.mean(v, axis=-1, keepdims=True)
    xc = v - mu
    var = jnp.mean(xc * xc, axis=-1, keepdims=True)
    y = _silu(xc * lax.rsqrt(var + EPS) * lng_ref[...] + lnb_ref[...])
    out = _dot(y.astype(BF16), w2_ref[...]) + b2_ref[...]
    o_ref[...] = x_ref[...] + mod_ref[G1:G1 + 1, :] * (_rms(out) * ng_ref[NG_POST_MIX:NG_POST_MIX + 1, :])


def _conv_out_call(geo, u, x, mod, ng, wdw, bdw, lng, lnb, w2, b2):
    m, d = x.shape
    tm = CONV_TM
    assert geo.S % tm == 0 and geo.T % tm == 0
    hb = tm // HALO
    nhb = m // HALO
    kern = functools.partial(_conv_out_kernel, geo.P // tm, geo.S // tm, geo.T // tm)
    vec = lambda a: a.reshape(1, d)
    return pl.pallas_call(
        kern,
        grid=(m // tm,),
        in_specs=[
            pl.BlockSpec((tm, d), lambda i: (i, 0)),
            pl.BlockSpec((HALO, d), lambda i: (jnp.maximum(i * hb - 1, 0), 0)),
            pl.BlockSpec((HALO, d), lambda i: (jnp.minimum((i + 1) * hb, nhb - 1), 0)),
            pl.BlockSpec((tm, d), lambda i: (i, 0)),
            pl.BlockSpec((None, N_MOD, d), lambda i: (geo.mod_row(i, tm), 0, 0)),
            pl.BlockSpec((4, d), lambda i: (0, 0)),
            pl.BlockSpec((CONV_W, d), lambda i: (0, 0)),
            pl.BlockSpec((1, d), lambda i: (0, 0)),
            pl.BlockSpec((1, d), lambda i: (0, 0)),
            pl.BlockSpec((1, d), lambda i: (0, 0)),
            pl.BlockSpec((d, d), lambda i: (0, 0)),
            pl.BlockSpec((1, d), lambda i: (0, 0)),
        ],
        out_specs=pl.BlockSpec((tm, d), lambda i: (i, 0)),
        out_shape=jax.ShapeDtypeStruct((m, d), F32),
        scratch_shapes=[pltpu.VMEM((tm + 2 * HALO, d), F32), pltpu.VMEM((tm, d), F32)],
        compiler_params=_cparams(("arbitrary",)),
        name="conv_out",
    )(u, u, u, x, mod, ng, wdw, vec(bdw), vec(lng), vec(lnb), w2, vec(b2))


def _ffn_kernel(x_ref, mod_ref, ng_ref, wg_ref, wu_ref, wd_ref, o_ref, h_scr, acc):
    f = pl.program_id(1)

    @pl.when(f == 0)
    def _():
        h = _normmod(x_ref[...], ng_ref[NG_PRE_FFN:NG_PRE_FFN + 1, :],
                     mod_ref[SC2:SC2 + 1, :], mod_ref[SH2:SH2 + 1, :])
        h_scr[...] = h.astype(BF16)
        acc[...] = jnp.zeros_like(acc)

    h = h_scr[...]
    t = _silu(_dot(h, wg_ref[...])) * _dot(h, wu_ref[...])
    acc[...] += _dot(t.astype(BF16), wd_ref[...])

    @pl.when(f == pl.num_programs(1) - 1)
    def _():
        o_ref[...] = x_ref[...] + mod_ref[G2:G2 + 1, :] * (
            _rms(acc[...]) * ng_ref[NG_POST_FFN:NG_POST_FFN + 1, :])


def _ffn_call(geo, x, mod, ng, wg, wu, wd):
    m, d = x.shape
    dff = wg.shape[1]
    tm, tf = 512, 512
    geo.check_tile(tm)
    return pl.pallas_call(
        _ffn_kernel,
        grid=(m // tm, dff // tf),
        in_specs=[
            pl.BlockSpec((tm, d), lambda i, f: (i, 0)),
            pl.BlockSpec((None, N_MOD, d), lambda i, f: (geo.mod_row(i, tm), 0, 0)),
            pl.BlockSpec((4, d), lambda i, f: (0, 0)),
            pl.BlockSpec((d, tf), lambda i, f: (0, f)),
            pl.BlockSpec((d, tf), lambda i, f: (0, f)),
            pl.BlockSpec((tf, d), lambda i, f: (f, 0)),
        ],
        out_specs=pl.BlockSpec((tm, d), lambda i, f: (i, 0)),
        out_shape=jax.ShapeDtypeStruct((m, d), F32),
        scratch_shapes=[pltpu.VMEM((tm, d), BF16), pltpu.VMEM((tm, d), F32)],
        compiler_params=_cparams(("arbitrary", "arbitrary")),
        name="ffn",
    )(x, mod, ng, wg, wu, wd)


def _qkv_kernel(hd, scale, x_ref, mod_ref, ng_ref, w_ref, gq_ref, gk_ref, cos_ref, sin_ref,
                q_ref, k_ref, v_ref):
    h = _normmod(x_ref[...], ng_ref[NG_PRE_MIX:NG_PRE_MIX + 1, :],
                 mod_ref[SC1:SC1 + 1, :], mod_ref[SH1:SH1 + 1, :])
    r = _dot(h.astype(BF16), w_ref[...])
    nq = q_ref.shape[1]
    nk = k_ref.shape[1]
    cos = cos_ref[...]
    sin = sin_ref[...]
    for c in range(0, nq, hd):
        qh = _rope(_rms(r[:, c:c + hd]) * gq_ref[...], cos, sin, hd // 4)
        q_ref[:, c:c + hd] = (qh * scale).astype(BF16)
    for c in range(0, nk, hd):
        k_ref[:, c:c + hd] = _rope(_rms(r[:, nq + c:nq + c + hd]) * gk_ref[...], cos, sin, hd // 4)
    v_ref[...] = r[:, nq + nk:]


def _qkv_call(geo, x, mod, ng, wqkv, gq, gk, cos, sin):
    m, d = x.shape
    hd = gq.shape[-1]
    assert hd == LANES
    nq, nk = N_HEADS * hd, N_KV_HEADS * hd
    tm = 512
    geo.check_tile(tm)
    kern = functools.partial(_qkv_kernel, hd, hd ** -0.5)
    return pl.pallas_call(
        kern,
        grid=(m // tm,),
        in_specs=[
            pl.BlockSpec((tm, d), lambda i: (i, 0)),
            pl.BlockSpec((None, N_MOD, d), lambda i: (geo.mod_row(i, tm), 0, 0)),
            pl.BlockSpec((4, d), lambda i: (0, 0)),
            pl.BlockSpec((d, nq + 2 * nk), lambda i: (0, 0)),
            pl.BlockSpec((1, hd), lambda i: (0, 0)),
            pl.BlockSpec((1, hd), lambda i: (0, 0)),
            pl.BlockSpec((tm, LANES), lambda i: (geo.rope_blk(i, tm), 0)),
            pl.BlockSpec((tm, LANES), lambda i: (geo.rope_blk(i, tm), 0)),
        ],
        out_specs=[
            pl.BlockSpec((tm, nq), lambda i: (i, 0)),
            pl.BlockSpec((tm, nk), lambda i: (i, 0)),
            pl.BlockSpec((tm, nk), lambda i: (i, 0)),
        ],
        out_shape=[
            jax.ShapeDtypeStruct((m, nq), BF16),
            jax.ShapeDtypeStruct((m, nk), F32),
            jax.ShapeDtypeStruct((m, nk), F32),
        ],
        compiler_params=_cparams(("arbitrary",)),
        name="qkv",
    )(x, mod, ng, wqkv, gq.reshape(1, hd), gk.reshape(1, hd), cos, sin)


def _attn_kernel(n_group, dk, dv, has_cache, *refs):
    if has_cache:
        q_ref, kn_ref, vn_ref, kc_ref, vc_ref, o_ref, kn_s, vn_s, kc_s, vc_s = refs
    else:
        q_ref, kn_ref, vn_ref, o_ref, kn_s, vn_s = refs

    @pl.when(pl.program_id(2) == 0)
    def _():
        kn_s[...] = kn_ref[...].astype(BF16)
        vn_s[...] = vn_ref[...].astype(BF16)
        if has_cache:
            kc_s[...] = kc_ref[...].astype(BF16)
            vc_s[...] = vc_ref[...].astype(BF16)

    nt = (((1,), (1,)), ((), ()))
    for g in range(n_group):
        q = q_ref[:, g * dk:(g + 1) * dk]
        s_n = lax.dot_general(q, kn_s[...], nt, preferred_element_type=F32)
        mx = jnp.max(s_n, axis=-1, keepdims=True)
        if has_cache:
            s_c = lax.dot_general(q, kc_s[...], nt, preferred_element_type=F32)
            mx = jnp.maximum(mx, jnp.max(s_c, axis=-1, keepdims=True))
        p_n = jnp.exp(s_n - mx)
        den = jnp.sum(p_n, axis=-1, keepdims=True)
        o = _dot(p_n.astype(BF16), vn_s[...])
        if has_cache:
            p_c = jnp.exp(s_c - mx)
            den = den + jnp.sum(p_c, axis=-1, keepdims=True)
            o = o + _dot(p_c.astype(BF16), vc_s[...])
        o_ref[:, g * dv:(g + 1) * dv] = (o / den).astype(BF16)


def _attn_call(q, kn, vn, kc, vc, *, n_batch, seq, row0, n_kv, n_group, dk, dv, tq, name):
    has_cache = kc is not None
    assert seq % tq == 0 and row0 % seq == 0
    nqt = seq // tq
    qb0, kb0 = row0 // tq, row0 // seq
    in_specs = [
        pl.BlockSpec((tq, n_group * dk), lambda b, h, t: (qb0 + b * nqt + t, h)),
        pl.BlockSpec((seq, dk), lambda b, h, t: (kb0 + b, h)),
        pl.BlockSpec((seq, dv), lambda b, h, t: (kb0 + b, h)),
    ]
    scratch = [pltpu.VMEM((seq, dk), BF16), pltpu.VMEM((seq, dv), BF16)]
    args = [q, kn, vn]
    if has_cache:
        past = kc.shape[1]
        in_specs += [
            pl.BlockSpec((None, past, dk), lambda b, h, t: (b, 0, h)),
            pl.BlockSpec((None, past, dv), lambda b, h, t: (b, 0, h)),
        ]
        scratch += [pltpu.VMEM((past, dk), BF16), pltpu.VMEM((past, dv), BF16)]
        args += [kc, vc]
    kern = functools.partial(_attn_kernel, n_group, dk, dv, has_cache)
    return pl.pallas_call(
        kern,
        grid=(n_batch, n_kv, nqt),
        in_specs=in_specs,
        out_specs=pl.BlockSpec((tq, n_group * dv), lambda b, h, t: (b * nqt + t, h)),
        out_shape=jax.ShapeDtypeStruct((n_batch * seq, n_kv * n_group * dv), BF16),
        scratch_shapes=scratch,
        compiler_params=_cparams(("arbitrary", "arbitrary", "arbitrary")),
        name=name,
    )(*args)


def _attn_out_kernel(x_ref, o_ref, mod_ref, ng_ref, wo_ref, y_ref):
    out = _dot(o_ref[...], wo_ref[...])
    y_ref[...] = x_ref[...] + mod_ref[G1:G1 + 1, :] * (
        _rms(out) * ng_ref[NG_POST_MIX:NG_POST_MIX + 1, :])


def _attn_out_call(geo, x, o, mod, ng, wo):
    m, d = x.shape
    tm = 512
    geo.check_tile(tm)
    return pl.pallas_call(
        _attn_out_kernel,
        grid=(m // tm,),
        in_specs=[
            pl.BlockSpec((tm, d), lambda i: (i, 0)),
            pl.BlockSpec((tm, o.shape[1]), lambda i: (i, 0)),
            pl.BlockSpec((None, N_MOD, d), lambda i: (geo.mod_row(i, tm), 0, 0)),
            pl.BlockSpec((4, d), lambda i: (0, 0)),
            pl.BlockSpec(wo.shape, lambda i: (0, 0)),
        ],
        out_specs=pl.BlockSpec((tm, d), lambda i: (i, 0)),
        out_shape=jax.ShapeDtypeStruct((m, d), F32),
        compiler_params=_cparams(("arbitrary",)),
        name="attn_out",
    )(x, o, mod, ng, wo)


def _route_kernel(n_exp, x_ref, mod_ref, ng_ref, wr_ref, br_ref, h_ref, gate_ref):
    h = _normmod(x_ref[...], ng_ref[NG_PRE_FFN:NG_PRE_FFN + 1, :],
                 mod_ref[SC2:SC2 + 1, :], mod_ref[SH2:SH2 + 1, :])
    h_ref[...] = h.astype(BF16)
    logits = jnp.dot(h, wr_ref[...], preferred_element_type=F32,
                     precision=lax.Precision.HIGHEST) + br_ref[...]
    lane = lax.broadcasted_iota(jnp.int32, logits.shape, 1)
    neg = jnp.float32(-jnp.inf)
    logits = jnp.where(lane < n_exp, logits, neg)
    m1 = jnp.max(logits, axis=-1, keepdims=True)
    i1 = jnp.min(jnp.where(logits == m1, lane, LANES), axis=-1, keepdims=True)
    rest = jnp.where(lane == i1, neg, logits)
    m2 = jnp.max(rest, axis=-1, keepdims=True)
    i2 = jnp.min(jnp.where(rest == m2, lane, LANES), axis=-1, keepdims=True)
    e = jnp.exp(m2 - m1)
    w1 = 1.0 / (1.0 + e)
    w2 = e / (1.0 + e)
    gate_ref[...] = jnp.where(lane == i1, w1, 0.0) + jnp.where(lane == i2, w2, 0.0)


def _route_call(geo, x, mod, ng, wr_pad, br_pad):
    m, d = x.shape
    tm = 512
    geo.check_tile(tm)
    kern = functools.partial(_route_kernel, N_EXPERTS)
    return pl.pallas_call(
        kern,
        grid=(m // tm,),
        in_specs=[
            pl.BlockSpec((tm, d), lambda i: (i, 0)),
            pl.BlockSpec((None, N_MOD, d), lambda i: (geo.mod_row(i, tm), 0, 0)),
            pl.BlockSpec((4, d), lambda i: (0, 0)),
            pl.BlockSpec((d, LANES), lambda i: (0, 0)),
            pl.BlockSpec((1, LANES), lambda i: (0, 0)),
        ],
        out_specs=[
            pl.BlockSpec((tm, d), lambda i: (i, 0)),
            pl.BlockSpec((tm, LANES), lambda i: (i, 0)),
        ],
        out_shape=[
            jax.ShapeDtypeStruct((m, d), BF16),
            jax.ShapeDtypeStruct((m, LANES), F32),
        ],
        compiler_params=_cparams(("arbitrary",)),
        name="route",
    )(x, mod, ng, wr_pad, br_pad)


def _moe_kernel(x_ref, h_ref, gate_ref, mod_ref, ng_ref, wg_ref, wu_ref, wd_ref, o_ref, acc):
    e = pl.program_id(1)
    f = pl.program_id(2)

    @pl.when((e == 0) & (f == 0))
    def _():
        acc[...] = jnp.zeros_like(acc)

    gates = gate_ref[...]
    lane = lax.broadcasted_iota(jnp.int32, gates.shape, 1)
    ge = jnp.sum(jnp.where(lane == e, gates, 0.0), axis=-1, keepdims=True)
    h = h_ref[...]
    t = _silu(_dot(h, wg_ref[...])) * _dot(h, wu_ref[...]) * ge
    acc[...] += _dot(t.astype(BF16), wd_ref[...])

    @pl.when((e == pl.num_programs(1) - 1) & (f == pl.num_programs(2) - 1))
    def _():
        o_ref[...] = x_ref[...] + mod_ref[G2:G2 + 1, :] * (
            _rms(acc[...]) * ng_ref[NG_POST_FFN:NG_POST_FFN + 1, :])


def _moe_call(geo, x, h, gates, mod, ng, wg, wu, wd):
    m, d = x.shape
    n_exp, _, dfe = wg.shape
    tm, tf = 512, 256
    geo.check_tile(tm)
    assert dfe % tf == 0
    return pl.pallas_call(
        _moe_kernel,
        grid=(m // tm, n_exp, dfe // tf),
        in_specs=[
            pl.BlockSpec((tm, d), lambda i, e, f: (i, 0)),
            pl.BlockSpec((tm, d), lambda i, e, f: (i, 0)),
            pl.BlockSpec((tm, LANES), lambda i, e, f: (i, 0)),
            pl.BlockSpec((None, N_MOD, d), lambda i, e, f: (geo.mod_row(i, tm), 0, 0)),
            pl.BlockSpec((4, d), lambda i, e, f: (0, 0)),
            pl.BlockSpec((None, d, tf), lambda i, e, f: (e, 0, f)),
            pl.BlockSpec((None, d, tf), lambda i, e, f: (e, 0, f)),
            pl.BlockSpec((None, tf, d), lambda i, e, f: (e, f, 0)),
        ],
        out_specs=pl.BlockSpec((tm, d), lambda i, e, f: (i, 0)),
        out_shape=jax.ShapeDtypeStruct((m, d), F32),
        scratch_shapes=[pltpu.VMEM((tm, d), F32)],
        compiler_params=_cparams(("arbitrary", "arbitrary", "arbitrary")),
        name="moe",
    )(x, h, gates, mod, ng, wg, wu, wd)


def _mla_in_kernel(scale, x_ref, mod_ref, ng_ref, wd_ref, gq_ref, gkv_ref, wuq_ref, wuk_ref, wuv_ref,
                   cos_ref, sin_ref, q_ref, kcat_ref, v_ref, ckv_ref, kpe_ref):
    h = _normmod(x_ref[...], ng_ref[NG_PRE_MIX:NG_PRE_MIX + 1, :],
                 mod_ref[SC1:SC1 + 1, :], mod_ref[SH1:SH1 + 1, :])
    r = _dot(h.astype(BF16), wd_ref[...])
    qr = gq_ref.shape[1]
    kr = gkv_ref.shape[1]
    rope_dim = kpe_ref.shape[1]
    cos = cos_ref[...]
    sin = sin_ref[...]
    cq = (_rms(r[:, :qr]) * gq_ref[...]).astype(BF16)
    ckv = _rms(r[:, qr:qr + kr]) * gkv_ref[...]
    kpe = r[:, qr + kr:qr + kr + LANES]
    ckv_ref[...] = ckv
    kpe_ref[...] = kpe[:, :rope_dim]
    ckv_b = ckv.astype(BF16)
    q = _dot(cq, wuq_ref[...])
    kn = _dot(ckv_b, wuk_ref[...])
    v_ref[...] = _dot(ckv_b, wuv_ref[...]).astype(BF16)
    kpe_r = _rope(kpe, cos, sin, rope_dim // 4).astype(BF16)
    for hh in range(q_ref.shape[1] // MLA_QK):
        c = hh * MLA_QK
        q_ref[:, c:c + NOPE_DIM] = (q[:, c:c + NOPE_DIM] * scale).astype(BF16)
        q_ref[:, c + NOPE_DIM:c + MLA_QK] = (
            _rope(q[:, c + NOPE_DIM:c + MLA_QK], cos, sin, rope_dim // 4) * scale).astype(BF16)
        kcat_ref[:, c:c + NOPE_DIM] = kn[:, hh * NOPE_DIM:(hh + 1) * NOPE_DIM].astype(BF16)
        kcat_ref[:, c + NOPE_DIM:c + MLA_QK] = kpe_r


def _mla_in_call(geo, x, mod, ng, wd_cat, gq, gkv, wuq_pad, wuk, wuv, cos, sin):
    m, d = x.shape
    qr, kr = gq.shape[-1], gkv.shape[-1]
    nqk = wuq_pad.shape[1]
    nv = wuv.shape[1]
    tm = 512
    geo.check_tile(tm)
    kern = functools.partial(_mla_in_kernel, (NOPE_DIM + ROPE_DIM) ** -0.5)
    full = lambda a: pl.BlockSpec(a.shape, lambda i: (0, 0))
    row = lambda n: pl.BlockSpec((tm, n), lambda i: (i, 0))
    return pl.pallas_call(
        kern,
        grid=(m // tm,),
        in_specs=[
            row(d),
            pl.BlockSpec((None, N_MOD, d), lambda i: (geo.mod_row(i, tm), 0, 0)),
            pl.BlockSpec((4, d), lambda i: (0, 0)),
            full(wd_cat),
            pl.BlockSpec((1, qr), lambda i: (0, 0)),
            pl.BlockSpec((1, kr), lambda i: (0, 0)),
            full(wuq_pad), full(wuk), full(wuv),
            pl.BlockSpec((tm, LANES), lambda i: (geo.rope_blk(i, tm), 0)),
            pl.BlockSpec((tm, LANES), lambda i: (geo.rope_blk(i, tm), 0)),
        ],
        out_specs=[row(nqk), row(nqk), row(nv), row(kr), row(ROPE_DIM)],
        out_shape=[
            jax.ShapeDtypeStruct((m, nqk), BF16),
            jax.ShapeDtypeStruct((m, nqk), BF16),
            jax.ShapeDtypeStruct((m, nv), BF16),
            jax.ShapeDtypeStruct((m, kr), F32),
            jax.ShapeDtypeStruct((m, ROPE_DIM), F32),
        ],
        compiler_params=_cparams(("arbitrary",)),
        name="mla_in",
    )(x, mod, ng, wd_cat, gq.reshape(1, qr), gkv.reshape(1, kr), wuq_pad, wuk, wuv, cos, sin)


def _mla_cache_kernel(ckv_ref, kpe_ref, wuk_ref, wuv_ref, kcat_ref, v_ref):
    ckv = ckv_ref[...].astype(BF16)
    kn = _dot(ckv, wuk_ref[...])
    v_ref[...] = _dot(ckv, wuv_ref[...]).astype(BF16)
    kpe = kpe_ref[...].astype(BF16)
    for hh in range(kcat_ref.shape[1] // MLA_QK):
        c = hh * MLA_QK
        kcat_ref[:, c:c + NOPE_DIM] = kn[:, hh * NOPE_DIM:(hh + 1) * NOPE_DIM].astype(BF16)
        kcat_ref[:, c + NOPE_DIM:c + MLA_QK] = kpe


def _mla_cache_call(ckv, kpe_pad, wuk, wuv):
    n, kr = ckv.shape
    tm = 512
    assert n % tm == 0
    nqk = MLA_HEADS * MLA_QK
    nv = wuv.shape[1]
    return pl.pallas_call(
        _mla_cache_kernel,
        grid=(n // tm,),
        in_specs=[
            pl.BlockSpec((tm, kr), lambda i: (i, 0)),
            pl.BlockSpec((tm, LANES), lambda i: (i, 0)),
            pl.BlockSpec(wuk.shape, lambda i: (0, 0)),
            pl.BlockSpec(wuv.shape, lambda i: (0, 0)),
        ],
        out_specs=[pl.BlockSpec((tm, nqk), lambda i: (i, 0)), pl.BlockSpec((tm, nv), lambda i: (i, 0))],
        out_shape=[jax.ShapeDtypeStruct((n, nqk), BF16), jax.ShapeDtypeStruct((n, nv), BF16)],
        compiler_params=_cparams(("arbitrary",)),
        name="mla_cache",
    )(ckv, kpe_pad, wuk, wuv)


def _rope_table(n_tokens, dim, n_ident):
    rows = n_tokens // GRID_W
    row = jnp.repeat(jnp.arange(rows, dtype=F32), GRID_W)
    col = jnp.tile(jnp.arange(GRID_W, dtype=F32), rows)
    half = dim // 2
    inv = ROPE_THETA ** (-jnp.arange(0, half, 2, dtype=F32) / half)
    ar = row[:, None] * inv[None, :]
    ac = col[:, None] * inv[None, :]
    ang = jnp.concatenate([ar, ar, ac, ac], axis=-1)
    cos = jnp.cos(ang)
    sin = jnp.sin(ang)
    hb = dim // 4
    first = (jnp.arange(dim) % (2 * hb)) < hb
    sin = jnp.where(first[None, :], -sin, sin)
    cos = jnp.pad(cos, ((0, n_ident), (0, LANES - dim)), constant_values=1.0)
    sin = jnp.pad(sin, ((0, n_ident), (0, LANES - dim)))
    return cos, sin


def kernel(x_prompt, x_sample, cache_gqa_k, cache_gqa_v, cache_mla_ckv, cache_mla_kpe, c, c_ctx,
           w_mod, b_mod, norm_g,
           conv_w1, conv_b1, conv_wdw, conv_bdw, conv_ln_g, conv_ln_b, conv_w2, conv_b2,
           gqa_wq, gqa_wk, gqa_wv, gqa_gq, gqa_gk, gqa_wo,
           mla_wdq, mla_gq, mla_wuq, mla_wdkv, mla_gkv, mla_wuk, mla_wuv, mla_wo,
           ffn_wg, ffn_wu, ffn_wd,
           moe_wr, moe_br, moe_wg, moe_wu, moe_wd):
    b, s, d = x_prompt.shape
    db, t, _ = x_sample.shape
    geo = _Geom(b, s, db, t)
    depth = w_mod.shape[0]
    past = cache_gqa_k.shape[2]
    hd = gqa_gq.shape[-1]
    bf = lambda a: a.astype(BF16)

    x = jnp.concatenate([x_prompt.reshape(geo.P, d), x_sample.reshape(db * t, d)], axis=0)

    n_cond = 1 + db
    cond = jnp.concatenate([c_ctx[None, :], c], axis=0)
    cond = jnp.pad(cond, ((0, -n_cond % 8), (0, 0)))
    mods = _mod_call(cond, w_mod, b_mod).reshape(depth, cond.shape[0], N_MOD, d)

    rope_tm = 512
    cos_hd, sin_hd = _rope_table(t, hd, rope_tm)
    cos_pe, sin_pe = _rope_table(t, ROPE_DIM, rope_tm)

    ks, vs, cs, ps = [], [], [], []
    for i in range(depth):
        mod, ng = mods[i], norm_g[i]
        j = i // N_MIXERS
        if i % N_MIXERS == 0:
            u = _conv_in_call(geo, x, mod, ng, bf(conv_w1[j]), conv_b1[j])
            x = _conv_out_call(geo, u, x, mod, ng, conv_wdw[j], conv_bdw[j], conv_ln_g[j],
                               conv_ln_b[j], bf(conv_w2[j]), conv_b2[j])
        elif i % N_MIXERS == 1:
            wqkv = bf(jnp.concatenate([gqa_wq[j], gqa_wk[j], gqa_wv[j]], axis=1))
            q, k, v = _qkv_call(geo, x, mod, ng, wqkv, gqa_gq[j], gqa_gk[j], cos_hd, sin_hd)
            ks.append(k[:geo.P].reshape(b, s, N_KV_HEADS, hd))
            vs.append(v[:geo.P].reshape(b, s, N_KV_HEADS, hd))
            kw = dict(n_kv=N_KV_HEADS, n_group=N_HEADS // N_KV_HEADS, dk=hd, dv=hd)
            o_p = _attn_call(q, k, v, None, None, n_batch=b, seq=s, row0=0, tq=min(s, 256),
                             name="gqa_attn_prompt", **kw)
            o_s = _attn_call(q, k, v,
                             cache_gqa_k[:, j].reshape(db, past, N_KV_HEADS * hd),
                             cache_gqa_v[:, j].reshape(db, past, N_KV_HEADS * hd),
                             n_batch=db, seq=t, row0=geo.P, tq=256, name="gqa_attn_sample", **kw)
            x = _attn_out_call(geo, x, jnp.concatenate([o_p, o_s], axis=0), mod, ng, bf(gqa_wo[j]))
        else:
            qr, kr = mla_gq.shape[-1], mla_gkv.shape[-1]
            wd_cat = bf(jnp.pad(jnp.concatenate([mla_wdq[j], mla_wdkv[j]], axis=1),
                                ((0, 0), (0, LANES - ROPE_DIM))))
            wuq_pad = bf(jnp.pad(mla_wuq[j], ((0, 0), (0, 0), (0, MLA_QK - NOPE_DIM - ROPE_DIM)))
                         ).reshape(qr, MLA_HEADS * MLA_QK)
            wuk = bf(mla_wuk[j]).reshape(kr, MLA_HEADS * NOPE_DIM)
            wuv = bf(mla_wuv[j]).reshape(kr, MLA_HEADS * V_DIM)
            q, kcat, v, ckv, kpe = _mla_in_call(geo, x, mod, ng, wd_cat, mla_gq[j], mla_gkv[j],
                                                wuq_pad, wuk, wuv, cos_pe, sin_pe)
            cs.append(ckv[:geo.P].reshape(b, s, kr))
            ps.append(kpe[:geo.P].reshape(b, s, ROPE_DIM))
            kc, vc = _mla_cache_call(
                cache_mla_ckv[:, j].reshape(db * past, kr),
                jnp.pad(cache_mla_kpe[:, j].reshape(db * past, ROPE_DIM), ((0, 0), (0, LANES - ROPE_DIM))),
                wuk, wuv)
            kw = dict(n_kv=MLA_HEADS, n_group=1, dk=MLA_QK, dv=V_DIM)
            o_p = _attn_call(q, kcat, v, None, None, n_batch=b, seq=s, row0=0, tq=min(s, 256),
                             name="mla_attn_prompt", **kw)
            o_s = _attn_call(q, kcat, v, kc.reshape(db, past, -1), vc.reshape(db, past, -1),
                             n_batch=db, seq=t, row0=geo.P, tq=512, name="mla_attn_sample", **kw)
            x = _attn_out_call(geo, x, jnp.concatenate([o_p, o_s], axis=0), mod, ng, bf(mla_wo[j]))
        f = i // 2
        if i % 2 == 0:
            x = _ffn_call(geo, x, mod, ng, bf(ffn_wg[f]), bf(ffn_wu[f]), bf(ffn_wd[f]))
        else:
            wr_pad = jnp.pad(moe_wr[f], ((0, 0), (0, LANES - N_EXPERTS)))
            br_pad = jnp.pad(moe_br[f], (0, LANES - N_EXPERTS)).reshape(1, LANES)
            h, gates = _route_call(geo, x, mod, ng, wr_pad, br_pad)
            x = _moe_call(geo, x, h, gates, mod, ng, bf(moe_wg[f]), bf(moe_wu[f]), bf(moe_wd[f]))

    y_prompt = x[:geo.P].reshape(b, s, d)
    y_sample = x[geo.P:].reshape(db, t, d)
    return (y_prompt, y_sample, jnp.stack(ks, axis=1), jnp.stack(vs, axis=1),
            jnp.stack(cs, axis=1), jnp.stack(ps, axis=1))
```

```python
import functools

import jax
import jax.numpy as jnp
from jax import lax
from jax.experimental import pallas as pl
from jax.experimental.pallas import tpu as pltpu

F32 = jnp.float32
BF16 = jnp.bfloat16

EPS = 1e-6
GRID_W = 64
ROPE_THETA = 10000.0
N_MIXERS = 3
N_MOD = 6
CONV_W = 31
CONV_PAD = CONV_W // 2
N_HEADS = 16
N_KV_HEADS = 4
MLA_HEADS = 16
NOPE_DIM = 128
ROPE_DIM = 64
V_DIM = 128
N_EXPERTS = 8
TOP_K = 2
MOE_TR = 512

LANES = 128
SUBLANES = 8
HALO = 16
MLA_QK = 256
VMEM_LIMIT = 56 * 1024 * 1024

SH1, SC1, G1, SH2, SC2, G2 = range(6)
NG_PRE_MIX, NG_POST_MIX, NG_PRE_FFN, NG_POST_FFN = range(4)


def _cparams(sem):
    return pltpu.CompilerParams(dimension_semantics=sem, vmem_limit_bytes=VMEM_LIMIT)


def _rms(x):
    return x * lax.rsqrt(jnp.mean(x * x, axis=-1, keepdims=True) + EPS)


def _normmod(x, g, sc, sh):
    return (_rms(x) * g) * (1.0 + sc) + sh


def _silu(x):
    return x * jax.nn.sigmoid(x)


def _dot(a, b):
    return jnp.dot(a, b, preferred_element_type=F32)


def _rope(x, cos, sin_signed, hb):
    lane = lax.broadcasted_iota(jnp.int32, x.shape, 1)
    first = (lane % (2 * hb)) < hb
    rot = jnp.where(first, pltpu.roll(x, LANES - hb, 1), pltpu.roll(x, hb, 1))
    return x * cos + rot * sin_signed


class _Geom:
    def __init__(self, b, s, db, t):
        self.B, self.S, self.DB, self.T = b, s, db, t
        self.P = b * s
        self.M = self.P + db * t

    def check_tile(self, tm):
        assert self.P % tm == 0 and self.T % tm == 0, (self.P, self.T, tm)

    def mod_row(self, i, tm):
        npt = self.P // tm
        return jnp.where(i < npt, 0, 1 + (i - npt) // (self.T // tm))

    def rope_blk(self, i, tm):
        npt = self.P // tm
        return jnp.where(i < npt, self.T // tm, (i - npt) % (self.T // tm))


def _mod_kernel(c_ref, w_ref, b_ref, o_ref):
    c = c_ref[...]
    o_ref[...] = _dot(_silu(c).astype(BF16), w_ref[...].astype(BF16)) + b_ref[...]


def _mod_call(cond, w_mod, b_mod):
    depth, d, n = w_mod.shape
    rows = cond.shape[0]
    tn = 1024
    return pl.pallas_call(
        _mod_kernel,
        grid=(depth, n // tn),
        in_specs=[
            pl.BlockSpec((rows, d), lambda l, j: (0, 0)),
            pl.BlockSpec((None, d, tn), lambda l, j: (l, 0, j)),
            pl.BlockSpec((None, 1, tn), lambda l, j: (l, 0, j)),
        ],
        out_specs=pl.BlockSpec((None, rows, tn), lambda l, j: (l, 0, j)),
        out_shape=jax.ShapeDtypeStruct((depth, rows, n), F32),
        compiler_params=_cparams(("arbitrary", "arbitrary")),
        name="mod",
    )(cond, w_mod, b_mod.reshape(depth, 1, n))


def _conv_in_kernel(x_ref, mod_ref, ng_ref, wa_ref, wg_ref, ba_ref, bg_ref, u_ref, h_scr):
    @pl.when(pl.program_id(1) == 0)
    def _():
        h = _normmod(x_ref[...], ng_ref[NG_PRE_MIX:NG_PRE_MIX + 1, :],
                     mod_ref[SC1:SC1 + 1, :], mod_ref[SH1:SH1 + 1, :])
        h_scr[...] = h.astype(BF16)

    h = h_scr[...]
    a = _dot(h, wa_ref[...]) + ba_ref[...]
    g = _dot(h, wg_ref[...]) + bg_ref[...]
    u_ref[...] = a * jax.nn.sigmoid(g)


def _conv_in_call(geo, x, mod, ng, w1, b1):
    m, d = x.shape
    tm, tn = 512, 1024
    geo.check_tile(tm)
    nj = d // tn
    b1 = b1.reshape(1, 2 * d)
    return pl.pallas_call(
        _conv_in_kernel,
        grid=(m // tm, nj),
        in_specs=[
            pl.BlockSpec((tm, d), lambda i, j: (i, 0)),
            pl.BlockSpec((None, N_MOD, d), lambda i, j: (geo.mod_row(i, tm), 0, 0)),
            pl.BlockSpec((4, d), lambda i, j: (0, 0)),
            pl.BlockSpec((d, tn), lambda i, j: (0, j)),
            pl.BlockSpec((d, tn), lambda i, j: (0, j + nj)),
            pl.BlockSpec((1, tn), lambda i, j: (0, j)),
            pl.BlockSpec((1, tn), lambda i, j: (0, j + nj)),
        ],
        out_specs=pl.BlockSpec((tm, tn), lambda i, j: (i, j)),
        out_shape=jax.ShapeDtypeStruct((m, d), F32),
        scratch_shapes=[pltpu.VMEM((tm, d), BF16)],
        compiler_params=_cparams(("arbitrary", "arbitrary")),
        name="conv_in",
    )(x, mod, ng, w1, w1, b1, b1)


CONV_TM = 256
CONV_RC = 32
CONV_CC = 256


def _conv_out_kernel(tiles_p, tps_p, tps_s,
                     u_ref, up_ref, un_ref, x_ref, mod_ref, ng_ref, wdw_ref, bdw_ref,
                     lng_ref, lnb_ref, w2_ref, b2_ref, o_ref, ext, cv, sh):
    i = pl.program_id(0)
    is_p = i < tiles_p
    j = jnp.where(is_p, i % tps_p, (i - tiles_p) % tps_s)
    n = jnp.where(is_p, tps_p, tps_s)
    tm, d = u_ref.shape
    ext[HALO:HALO + tm, :] = u_ref[...]
    ext[0:HALO, :] = jnp.where(j == 0, 0.0, up_ref[...])
    ext[HALO + tm:HALO + tm + HALO, :] = jnp.where(j == n - 1, 0.0, un_ref[...])

    def chunk(c, carry):
        c0 = pl.multiple_of(c * CONV_CC, CONV_CC)
        e = ext[:, pl.ds(c0, CONV_CC)]
        for p in range(1, SUBLANES):
            sh[p - 1] = pltpu.roll(e, e.shape[0] - p, 0)
        for r in range(0, tm, CONV_RC):
            acc = jnp.zeros((CONV_RC, CONV_CC), F32)
            for k in range(CONV_W):
                r0 = HALO - CONV_PAD + r + k
                p = r0 % SUBLANES
                a0 = r0 - p
                if p == 0:
                    tap = ext[a0:a0 + CONV_RC, pl.ds(c0, CONV_CC)]
                else:
                    tap = sh[p - 1, a0:a0 + CONV_RC, :]
                acc = acc + tap * wdw_ref[k:k + 1, pl.ds(c0, CONV_CC)]
            cv[r:r + CONV_RC, pl.ds(c0, CONV_CC)] = acc
        return carry

    lax.fori_loop(0, d // CONV_CC, chunk, 0)

    v = cv[...] + bdw_ref[...]
    mu = jnp.mean(v, axis=-1, keepdims=True)
    xc = v - mu
    var = jnp.mean(xc * xc, axis=-1, keepdims=True)
    y = _silu(xc * lax.rsqrt(var + EPS) * lng_ref[...] + lnb_ref[...])
    out = _dot(y.astype(BF16), w2_ref[...]) + b2_ref[...]
    o_ref[...] = x_ref[...] + mod_ref[G1:G1 + 1, :] * (_rms(out) * ng_ref[NG_POST_MIX:NG_POST_MIX + 1, :])


def _conv_out_call(geo, u, x, mod, ng, wdw, bdw, lng, lnb, w2, b2):
    m, d = x.shape
    tm = CONV_TM
    assert geo.S % tm == 0 and geo.T % tm == 0
    hb = tm // HALO
    nhb = m // HALO
    kern = functools.partial(_conv_out_kernel, geo.P // tm, geo.S // tm, geo.T // tm)
    vec = lambda a: a.reshape(1, d)
    return pl.pallas_call(
        kern,
        grid=(m // tm,),
        in_specs=[
            pl.BlockSpec((tm, d), lambda i: (i, 0)),
            pl.BlockSpec((HALO, d), lambda i: (jnp.maximum(i * hb - 1, 0), 0)),
            pl.BlockSpec((HALO, d), lambda i: (jnp.minimum((i + 1) * hb, nhb - 1), 0)),
            pl.BlockSpec((tm, d), lambda i: (i, 0)),
            pl.BlockSpec((None, N_MOD, d), lambda i: (geo.mod_row(i, tm), 0, 0)),
            pl.BlockSpec((4, d), lambda i: (0, 0)),
            pl.BlockSpec((CONV_W, d), lambda i: (0, 0)),
            pl.BlockSpec((1, d), lambda i: (0, 0)),
            pl.BlockSpec((1, d), lambda i: (0, 0)),
            pl.BlockSpec((1, d), lambda i: (0, 0)),
            pl.BlockSpec((d, d), lambda i: (0, 0)),
            pl.BlockSpec((1, d), lambda i: (0, 0)),
        ],
        out_specs=pl.BlockSpec((tm, d), lambda i: (i, 0)),
        out_shape=jax.ShapeDtypeStruct((m, d), F32),
        scratch_shapes=[pltpu.VMEM((tm + 2 * HALO, d), F32), pltpu.VMEM((tm, d), F32),
                        pltpu.VMEM((SUBLANES - 1, tm + 2 * HALO, CONV_CC), F32)],
        compiler_params=_cparams(("arbitrary",)),
        name="conv_out",
    )(u, u, u, x, mod, ng, wdw, vec(bdw), vec(lng), vec(lnb), w2, vec(b2))


def _ffn_kernel(x_ref, mod_ref, ng_ref, wg_ref, wu_ref, wd_ref, o_ref, h_scr, acc):
    f = pl.program_id(1)

    @pl.when(f == 0)
    def _():
        h = _normmod(x_ref[...], ng_ref[NG_PRE_FFN:NG_PRE_FFN + 1, :],
                     mod_ref[SC2:SC2 + 1, :], mod_ref[SH2:SH2 + 1, :])
        h_scr[...] = h.astype(BF16)
        acc[...] = jnp.zeros_like(acc)

    h = h_scr[...]
    t = _silu(_dot(h, wg_ref[...])) * _dot(h, wu_ref[...])
    acc[...] += _dot(t.astype(BF16), wd_ref[...])

    @pl.when(f == pl.num_programs(1) - 1)
    def _():
        o_ref[...] = x_ref[...] + mod_ref[G2:G2 + 1, :] * (
            _rms(acc[...]) * ng_ref[NG_POST_FFN:NG_POST_FFN + 1, :])


def _ffn_call(geo, x, mod, ng, wg, wu, wd):
    m, d = x.shape
    dff = wg.shape[1]
    tm, tf = 512, 512
    geo.check_tile(tm)
    return pl.pallas_call(
        _ffn_kernel,
        grid=(m // tm, dff // tf),
        in_specs=[
            pl.BlockSpec((tm, d), lambda i, f: (i, 0)),
            pl.BlockSpec((None, N_MOD, d), lambda i, f: (geo.mod_row(i, tm), 0, 0)),
            pl.BlockSpec((4, d), lambda i, f: (0, 0)),
            pl.BlockSpec((d, tf), lambda i, f: (0, f)),
            pl.BlockSpec((d, tf), lambda i, f: (0, f)),
            pl.BlockSpec((tf, d), lambda i, f: (f, 0)),
        ],
        out_specs=pl.BlockSpec((tm, d), lambda i, f: (i, 0)),
        out_shape=jax.ShapeDtypeStruct((m, d), F32),
        scratch_shapes=[pltpu.VMEM((tm, d), BF16), pltpu.VMEM((tm, d), F32)],
        compiler_params=_cparams(("arbitrary", "arbitrary")),
        name="ffn",
    )(x, mod, ng, wg, wu, wd)


def _qkv_kernel(hd, scale, x_ref, mod_ref, ng_ref, w_ref, gq_ref, gk_ref, cos_ref, sin_ref,
                q_ref, k_ref, v_ref):
    h = _normmod(x_ref[...], ng_ref[NG_PRE_MIX:NG_PRE_MIX + 1, :],
                 mod_ref[SC1:SC1 + 1, :], mod_ref[SH1:SH1 + 1, :])
    r = _dot(h.astype(BF16), w_ref[...])
    nq = q_ref.shape[1]
    nk = k_ref.shape[1]
    cos = cos_ref[...]
    sin = sin_ref[...]
    for c in range(0, nq, hd):
        qh = _rope(_rms(r[:, c:c + hd]) * gq_ref[...], cos, sin, hd // 4)
        q_ref[:, c:c + hd] = (qh * scale).astype(BF16)
    for c in range(0, nk, hd):
        k_ref[:, c:c + hd] = _rope(_rms(r[:, nq + c:nq + c + hd]) * gk_ref[...], cos, sin, hd // 4)
    v_ref[...] = r[:, nq + nk:]


def _qkv_call(geo, x, mod, ng, wqkv, gq, gk, cos, sin):
    m, d = x.shape
    hd = gq.shape[-1]
    assert hd == LANES
    nq, nk = N_HEADS * hd, N_KV_HEADS * hd
    tm = 512
    geo.check_tile(tm)
    kern = functools.partial(_qkv_kernel, hd, hd ** -0.5)
    return pl.pallas_call(
        kern,
        grid=(m // tm,),
        in_specs=[
            pl.BlockSpec((tm, d), lambda i: (i, 0)),
            pl.BlockSpec((None, N_MOD, d), lambda i: (geo.mod_row(i, tm), 0, 0)),
            pl.BlockSpec((4, d), lambda i: (0, 0)),
            pl.BlockSpec((d, nq + 2 * nk), lambda i: (0, 0)),
            pl.BlockSpec((1, hd), lambda i: (0, 0)),
            pl.BlockSpec((1, hd), lambda i: (0, 0)),
            pl.BlockSpec((tm, LANES), lambda i: (geo.rope_blk(i, tm), 0)),
            pl.BlockSpec((tm, LANES), lambda i: (geo.rope_blk(i, tm), 0)),
        ],
        out_specs=[
            pl.BlockSpec((tm, nq), lambda i: (i, 0)),
            pl.BlockSpec((tm, nk), lambda i: (i, 0)),
            pl.BlockSpec((tm, nk), lambda i: (i, 0)),
        ],
        out_shape=[
            jax.ShapeDtypeStruct((m, nq), BF16),
            jax.ShapeDtypeStruct((m, nk), F32),
            jax.ShapeDtypeStruct((m, nk), F32),
        ],
        compiler_params=_cparams(("arbitrary",)),
        name="qkv",
    )(x, mod, ng, wqkv, gq.reshape(1, hd), gk.reshape(1, hd), cos, sin)


def _attn_kernel(n_group, dk, dv, has_cache, *refs):
    if has_cache:
        q_ref, kn_ref, vn_ref, kc_ref, vc_ref, o_ref, kn_s, vn_s, kc_s, vc_s = refs
    else:
        q_ref, kn_ref, vn_ref, o_ref, kn_s, vn_s = refs

    @pl.when(pl.program_id(2) == 0)
    def _():
        kn_s[...] = kn_ref[...].astype(BF16)
        vn_s[...] = vn_ref[...].astype(BF16)
        if has_cache:
            kc_s[...] = kc_ref[...].astype(BF16)
            vc_s[...] = vc_ref[...].astype(BF16)

    nt = (((1,), (1,)), ((), ()))
    for g in range(n_group):
        q = q_ref[:, g * dk:(g + 1) * dk]
        s_n = lax.dot_general(q, kn_s[...], nt, preferred_element_type=F32)
        mx = jnp.max(s_n, axis=-1, keepdims=True)
        if has_cache:
            s_c = lax.dot_general(q, kc_s[...], nt, preferred_element_type=F32)
            mx = jnp.maximum(mx, jnp.max(s_c, axis=-1, keepdims=True))
        p_n = jnp.exp(s_n - mx)
        den = jnp.sum(p_n, axis=-1, keepdims=True)
        o = _dot(p_n.astype(BF16), vn_s[...])
        if has_cache:
            p_c = jnp.exp(s_c - mx)
            den = den + jnp.sum(p_c, axis=-1, keepdims=True)
            o = o + _dot(p_c.astype(BF16), vc_s[...])
        o_ref[:, g * dv:(g + 1) * dv] = (o / den).astype(BF16)


def _attn_call(q, kn, vn, kc, vc, *, n_batch, seq, row0, n_kv, n_group, dk, dv, tq, name):
    has_cache = kc is not None
    assert seq % tq == 0 and row0 % seq == 0
    nqt = seq // tq
    qb0, kb0 = row0 // tq, row0 // seq
    in_specs = [
        pl.BlockSpec((tq, n_group * dk), lambda b, h, t: (qb0 + b * nqt + t, h)),
        pl.BlockSpec((seq, dk), lambda b, h, t: (kb0 + b, h)),
        pl.BlockSpec((seq, dv), lambda b, h, t: (kb0 + b, h)),
    ]
    scratch = [pltpu.VMEM((seq, dk), BF16), pltpu.VMEM((seq, dv), BF16)]
    args = [q, kn, vn]
    if has_cache:
        past = kc.shape[1]
        in_specs += [
            pl.BlockSpec((None, past, dk), lambda b, h, t: (b, 0, h)),
            pl.BlockSpec((None, past, dv), lambda b, h, t: (b, 0, h)),
        ]
        scratch += [pltpu.VMEM((past, dk), BF16), pltpu.VMEM((past, dv), BF16)]
        args += [kc, vc]
    kern = functools.partial(_attn_kernel, n_group, dk, dv, has_cache)
    return pl.pallas_call(
        kern,
        grid=(n_batch, n_kv, nqt),
        in_specs=in_specs,
        out_specs=pl.BlockSpec((tq, n_group * dv), lambda b, h, t: (b * nqt + t, h)),
        out_shape=jax.ShapeDtypeStruct((n_batch * seq, n_kv * n_group * dv), BF16),
        scratch_shapes=scratch,
        compiler_params=_cparams(("arbitrary", "arbitrary", "arbitrary")),
        name=name,
    )(*args)


def _attn_out_kernel(npt, x_ref, op_ref, os_ref, mod_ref, ng_ref, wo_ref, y_ref):
    def project(o_ref):
        out = _dot(o_ref[...], wo_ref[...])
        y_ref[...] = x_ref[...] + mod_ref[G1:G1 + 1, :] * (
            _rms(out) * ng_ref[NG_POST_MIX:NG_POST_MIX + 1, :])

    i = pl.program_id(0)
    pl.when(i < npt)(lambda: project(op_ref))
    pl.when(i >= npt)(lambda: project(os_ref))


def _attn_out_call(geo, x, o_p, o_s, mod, ng, wo):
    m, d = x.shape
    tm = 512
    geo.check_tile(tm)
    npt = geo.P // tm
    n = o_p.shape[1]
    return pl.pallas_call(
        functools.partial(_attn_out_kernel, npt),
        grid=(m // tm,),
        in_specs=[
            pl.BlockSpec((tm, d), lambda i: (i, 0)),
            pl.BlockSpec((tm, n), lambda i: (jnp.minimum(i, npt - 1), 0)),
            pl.BlockSpec((tm, n), lambda i: (jnp.maximum(i - npt, 0), 0)),
            pl.BlockSpec((None, N_MOD, d), lambda i: (geo.mod_row(i, tm), 0, 0)),
            pl.BlockSpec((4, d), lambda i: (0, 0)),
            pl.BlockSpec(wo.shape, lambda i: (0, 0)),
        ],
        out_specs=pl.BlockSpec((tm, d), lambda i: (i, 0)),
        out_shape=jax.ShapeDtypeStruct((m, d), F32),
        compiler_params=_cparams(("arbitrary",)),
        name="attn_out",
    )(x, o_p, o_s, mod, ng, wo)


def _route_kernel(n_exp, x_ref, mod_ref, ng_ref, wr_ref, br_ref, h_ref, topw_ref, topi_ref):
    h = _normmod(x_ref[...], ng_ref[NG_PRE_FFN:NG_PRE_FFN + 1, :],
                 mod_ref[SC2:SC2 + 1, :], mod_ref[SH2:SH2 + 1, :])
    h_ref[...] = h
    logits = jnp.dot(h, wr_ref[...], preferred_element_type=F32,
                     precision=lax.Precision.HIGHEST) + br_ref[...]
    lane = lax.broadcasted_iota(jnp.int32, logits.shape, 1)
    neg = jnp.float32(-jnp.inf)
    logits = jnp.where(lane < n_exp, logits, neg)
    m1 = jnp.max(logits, axis=-1, keepdims=True)
    i1 = jnp.min(jnp.where(logits == m1, lane, LANES), axis=-1, keepdims=True)
    rest = jnp.where(lane == i1, neg, logits)
    m2 = jnp.max(rest, axis=-1, keepdims=True)
    i2 = jnp.min(jnp.where(rest == m2, lane, LANES), axis=-1, keepdims=True)
    e = jnp.exp(m2 - m1)
    w1 = 1.0 / (1.0 + e)
    w2 = e / (1.0 + e)
    topw_ref[...] = jnp.where(lane == 0, w1, jnp.where(lane == 1, w2, 0.0))
    topi_ref[...] = jnp.where(lane == 0, i1, jnp.where(lane == 1, i2, 0))


def _route_call(geo, x, mod, ng, wr_pad, br_pad):
    m, d = x.shape
    tm = 512
    geo.check_tile(tm)
    kern = functools.partial(_route_kernel, N_EXPERTS)
    return pl.pallas_call(
        kern,
        grid=(m // tm,),
        in_specs=[
            pl.BlockSpec((tm, d), lambda i: (i, 0)),
            pl.BlockSpec((None, N_MOD, d), lambda i: (geo.mod_row(i, tm), 0, 0)),
            pl.BlockSpec((4, d), lambda i: (0, 0)),
            pl.BlockSpec((d, LANES), lambda i: (0, 0)),
            pl.BlockSpec((1, LANES), lambda i: (0, 0)),
        ],
        out_specs=[
            pl.BlockSpec((tm, d), lambda i: (i, 0)),
            pl.BlockSpec((tm, LANES), lambda i: (i, 0)),
            pl.BlockSpec((tm, LANES), lambda i: (i, 0)),
        ],
        out_shape=[
            jax.ShapeDtypeStruct((m, d), F32),
            jax.ShapeDtypeStruct((m, LANES), F32),
            jax.ShapeDtypeStruct((m, LANES), jnp.int32),
        ],
        compiler_params=_cparams(("arbitrary",)),
        name="route",
    )(x, mod, ng, wr_pad, br_pad)


def _dispatch(topi, topw, tr):
    m = topi.shape[0]
    n_asg = TOP_K * m
    n_tiles = n_asg // tr + N_EXPERTS
    e = topi[:, :TOP_K].reshape(n_asg)
    w = topw[:, :TOP_K].reshape(n_asg)
    onehot = (e[:, None] == jnp.arange(N_EXPERTS, dtype=jnp.int32)[None, :]).astype(jnp.int32)
    csum = jnp.cumsum(onehot, axis=0)
    count = csum[-1]
    rank = jnp.sum(csum * onehot, axis=1) - 1
    ptiles = (count + tr - 1) // tr
    tile_end = jnp.cumsum(ptiles)
    tile_start = tile_end - ptiles
    n_used = tile_end[-1:]
    dest = tile_start[e] * tr + rank
    src = jnp.zeros((n_tiles * tr,), jnp.int32).at[dest].set(jnp.arange(n_asg, dtype=jnp.int32) // TOP_K)
    gate = jnp.zeros((n_tiles * tr,), F32).at[dest].set(w)
    tile_expert = jnp.minimum(
        jnp.sum((jnp.arange(n_tiles, dtype=jnp.int32)[:, None] >= tile_end[None, :]).astype(jnp.int32), axis=1),
        N_EXPERTS - 1)
    return src, gate, tile_expert, n_used.astype(jnp.int32), dest


def _row_copy(src_hbm, src_row, dst, dst_row, sem):
    return pltpu.make_async_copy(src_hbm.at[pl.ds(src_row, 1), :], dst.at[pl.ds(dst_row, 1), :], sem)


def _gather_rows(idx_ref, idx0, src_hbm, dst, n, sem):
    def start(r, c):
        _row_copy(src_hbm, idx_ref[0, idx0 + r], dst, r, sem).start()
        return c

    def wait(r, c):
        _row_copy(src_hbm, 0, dst, r, sem).wait()
        return c

    lax.fori_loop(0, n, start, 0)
    lax.fori_loop(0, n, wait, 0)


def _moe_kernel(te_ref, nu_ref, src_ref, h_hbm, g_ref, wg_ref, wu_ref, wd_ref, y_ref, hbuf, hbf, sem):
    i = pl.program_id(0)
    f = pl.program_id(1)
    used = i < nu_ref[0]

    @pl.when(used & (f == 0))
    def _():
        _gather_rows(src_ref, 0, h_hbm, hbuf, hbuf.shape[0], sem)
        hbf[...] = hbuf[...].astype(BF16)

    @pl.when(f == 0)
    def _():
        y_ref[...] = jnp.zeros_like(y_ref)

    @pl.when(used)
    def _():
        h = hbf[...]
        t = _silu(_dot(h, wg_ref[...])) * _dot(h, wu_ref[...]) * g_ref[...]
        y_ref[...] += _dot(t.astype(BF16), wd_ref[...])


def _moe_call(h, src, gate, tile_expert, n_used, wg, wu, wd, tr):
    m, d = h.shape
    n_exp, _, dfe = wg.shape
    n_tiles = tile_expert.shape[0]
    tf = 256
    assert dfe % tf == 0
    nf = dfe // tf

    def fblk(i, f, nu):
        return jnp.where(i < nu[0], f, nf - 1)

    grid_spec = pltpu.PrefetchScalarGridSpec(
        num_scalar_prefetch=2,
        grid=(n_tiles, nf),
        in_specs=[
            pl.BlockSpec((None, 1, tr), lambda i, f, te, nu: (i, 0, 0), memory_space=pltpu.SMEM),
            pl.BlockSpec(memory_space=pl.ANY),
            pl.BlockSpec((tr, 1), lambda i, f, te, nu: (i, 0)),
            pl.BlockSpec((None, d, tf), lambda i, f, te, nu: (te[i], 0, fblk(i, f, nu))),
            pl.BlockSpec((None, d, tf), lambda i, f, te, nu: (te[i], 0, fblk(i, f, nu))),
            pl.BlockSpec((None, tf, d), lambda i, f, te, nu: (te[i], fblk(i, f, nu), 0)),
        ],
        out_specs=pl.BlockSpec((tr, d), lambda i, f, te, nu: (i, 0)),
        scratch_shapes=[pltpu.VMEM((tr, d), F32), pltpu.VMEM((tr, d), BF16), pltpu.SemaphoreType.DMA(())],
    )
    return pl.pallas_call(
        _moe_kernel,
        grid_spec=grid_spec,
        out_shape=jax.ShapeDtypeStruct((n_tiles * tr, d), F32),
        compiler_params=_cparams(("arbitrary", "arbitrary")),
        name="moe",
    )(tile_expert, n_used, src.reshape(n_tiles, 1, tr), h, gate.reshape(n_tiles * tr, 1), wg, wu, wd)


def _combine_kernel(pos_ref, y_hbm, x_ref, mod_ref, ng_ref, o_ref, ya, yb, sem):
    tm = ya.shape[0]
    _gather_rows(pos_ref, 0, y_hbm, ya, tm, sem)
    _gather_rows(pos_ref, tm, y_hbm, yb, tm, sem)
    out = ya[...] + yb[...]
    o_ref[...] = x_ref[...] + mod_ref[G2:G2 + 1, :] * (
        _rms(out) * ng_ref[NG_POST_FFN:NG_POST_FFN + 1, :])


def _combine_call(geo, x, y, dest, mod, ng):
    m, d = x.shape
    tm = 256
    geo.check_tile(tm)
    nt = m // tm
    pos = dest.reshape(nt, tm, TOP_K).transpose(0, 2, 1).reshape(nt, 1, TOP_K * tm)
    return pl.pallas_call(
        _combine_kernel,
        grid=(nt,),
        in_specs=[
            pl.BlockSpec((None, 1, TOP_K * tm), lambda i: (i, 0, 0), memory_space=pltpu.SMEM),
            pl.BlockSpec(memory_space=pl.ANY),
            pl.BlockSpec((tm, d), lambda i: (i, 0)),
            pl.BlockSpec((None, N_MOD, d), lambda i: (geo.mod_row(i, tm), 0, 0)),
            pl.BlockSpec((4, d), lambda i: (0, 0)),
        ],
        out_specs=pl.BlockSpec((tm, d), lambda i: (i, 0)),
        out_shape=jax.ShapeDtypeStruct((m, d), F32),
        scratch_shapes=[pltpu.VMEM((tm, d), F32), pltpu.VMEM((tm, d), F32), pltpu.SemaphoreType.DMA(())],
        compiler_params=_cparams(("arbitrary",)),
        name="moe_combine",
    )(pos, y, x, mod, ng)


def _mla_in_kernel(scale, x_ref, mod_ref, ng_ref, wd_ref, gq_ref, gkv_ref, wuq_ref, wuk_ref, wuv_ref,
                   cos_ref, sin_ref, q_ref, kcat_ref, v_ref, ckv_ref, kpe_ref):
    h = _normmod(x_ref[...], ng_ref[NG_PRE_MIX:NG_PRE_MIX + 1, :],
                 mod_ref[SC1:SC1 + 1, :], mod_ref[SH1:SH1 + 1, :])
    r = _dot(h.astype(BF16), wd_ref[...])
    qr = gq_ref.shape[1]
    kr = gkv_ref.shape[1]
    rope_dim = kpe_ref.shape[1]
    cos = cos_ref[...]
    sin = sin_ref[...]
    cq = (_rms(r[:, :qr]) * gq_ref[...]).astype(BF16)
    ckv = _rms(r[:, qr:qr + kr]) * gkv_ref[...]
    kpe = r[:, qr + kr:qr + kr + LANES]
    ckv_ref[...] = ckv
    kpe_ref[...] = kpe[:, :rope_dim]
    ckv_b = ckv.astype(BF16)
    q = _dot(cq, wuq_ref[...])
    kn = _dot(ckv_b, wuk_ref[...])
    v_ref[...] = _dot(ckv_b, wuv_ref[...]).astype(BF16)
    kpe_r = _rope(kpe, cos, sin, rope_dim // 4).astype(BF16)
    for hh in range(q_ref.shape[1] // MLA_QK):
        c = hh * MLA_QK
        q_ref[:, c:c + NOPE_DIM] = (q[:, c:c + NOPE_DIM] * scale).astype(BF16)
        q_ref[:, c + NOPE_DIM:c + MLA_QK] = (
            _rope(q[:, c + NOPE_DIM:c + MLA_QK], cos, sin, rope_dim // 4) * scale).astype(BF16)
        kcat_ref[:, c:c + NOPE_DIM] = kn[:, hh * NOPE_DIM:(hh + 1) * NOPE_DIM].astype(BF16)
        kcat_ref[:, c + NOPE_DIM:c + MLA_QK] = kpe_r


def _mla_in_call(geo, x, mod, ng, wd_cat, gq, gkv, wuq_pad, wuk, wuv, cos, sin):
    m, d = x.shape
    qr, kr = gq.shape[-1], gkv.shape[-1]
    nqk = wuq_pad.shape[1]
    nv = wuv.shape[1]
    tm = 512
    geo.check_tile(tm)
    kern = functools.partial(_mla_in_kernel, (NOPE_DIM + ROPE_DIM) ** -0.5)
    full = lambda a: pl.BlockSpec(a.shape, lambda i: (0, 0))
    row = lambda n: pl.BlockSpec((tm, n), lambda i: (i, 0))
    return pl.pallas_call(
        kern,
        grid=(m // tm,),
        in_specs=[
            row(d),
            pl.BlockSpec((None, N_MOD, d), lambda i: (geo.mod_row(i, tm), 0, 0)),
            pl.BlockSpec((4, d), lambda i: (0, 0)),
            full(wd_cat),
            pl.BlockSpec((1, qr), lambda i: (0, 0)),
            pl.BlockSpec((1, kr), lambda i: (0, 0)),
            full(wuq_pad), full(wuk), full(wuv),
            pl.BlockSpec((tm, LANES), lambda i: (geo.rope_blk(i, tm), 0)),
            pl.BlockSpec((tm, LANES), lambda i: (geo.rope_blk(i, tm), 0)),
        ],
        out_specs=[row(nqk), row(nqk), row(nv), row(kr), row(ROPE_DIM)],
        out_shape=[
            jax.ShapeDtypeStruct((m, nqk), BF16),
            jax.ShapeDtypeStruct((m, nqk), BF16),
            jax.ShapeDtypeStruct((m, nv), BF16),
            jax.ShapeDtypeStruct((m, kr), F32),
            jax.ShapeDtypeStruct((m, ROPE_DIM), F32),
        ],
        compiler_params=_cparams(("arbitrary",)),
        name="mla_in",
    )(x, mod, ng, wd_cat, gq.reshape(1, qr), gkv.reshape(1, kr), wuq_pad, wuk, wuv, cos, sin)


def _mla_cache_kernel(ckv_ref, kpe_ref, wuk_ref, wuv_ref, kcat_ref, v_ref):
    ckv = ckv_ref[...].astype(BF16)
    kn = _dot(ckv, wuk_ref[...])
    v_ref[...] = _dot(ckv, wuv_ref[...]).astype(BF16)
    kpe = kpe_ref[...].astype(BF16)
    for hh in range(kcat_ref.shape[1] // MLA_QK):
        c = hh * MLA_QK
        kcat_ref[:, c:c + NOPE_DIM] = kn[:, hh * NOPE_DIM:(hh + 1) * NOPE_DIM].astype(BF16)
        kcat_ref[:, c + NOPE_DIM:c + MLA_QK] = kpe


def _mla_cache_call(ckv, kpe_pad, wuk, wuv):
    n, kr = ckv.shape
    tm = 512
    assert n % tm == 0
    nqk = MLA_HEADS * MLA_QK
    nv = wuv.shape[1]
    return pl.pallas_call(
        _mla_cache_kernel,
        grid=(n // tm,),
        in_specs=[
            pl.BlockSpec((tm, kr), lambda i: (i, 0)),
            pl.BlockSpec((tm, LANES), lambda i: (i, 0)),
            pl.BlockSpec(wuk.shape, lambda i: (0, 0)),
            pl.BlockSpec(wuv.shape, lambda i: (0, 0)),
        ],
        out_specs=[pl.BlockSpec((tm, nqk), lambda i: (i, 0)), pl.BlockSpec((tm, nv), lambda i: (i, 0))],
        out_shape=[jax.ShapeDtypeStruct((n, nqk), BF16), jax.ShapeDtypeStruct((n, nv), BF16)],
        compiler_params=_cparams(("arbitrary",)),
        name="mla_cache",
    )(ckv, kpe_pad, wuk, wuv)


def _rope_table(n_tokens, dim, n_ident):
    rows = n_tokens // GRID_W
    row = jnp.repeat(jnp.arange(rows, dtype=F32), GRID_W)
    col = jnp.tile(jnp.arange(GRID_W, dtype=F32), rows)
    half = dim // 2
    inv = ROPE_THETA ** (-jnp.arange(0, half, 2, dtype=F32) / half)
    ar = row[:, None] * inv[None, :]
    ac = col[:, None] * inv[None, :]
    ang = jnp.concatenate([ar, ar, ac, ac], axis=-1)
    cos = jnp.cos(ang)
    sin = jnp.sin(ang)
    hb = dim // 4
    first = (jnp.arange(dim) % (2 * hb)) < hb
    sin = jnp.where(first[None, :], -sin, sin)
    cos = jnp.pad(cos, ((0, n_ident), (0, LANES - dim)), constant_values=1.0)
    sin = jnp.pad(sin, ((0, n_ident), (0, LANES - dim)))
    return cos, sin


def kernel(x_prompt, x_sample, cache_gqa_k, cache_gqa_v, cache_mla_ckv, cache_mla_kpe, c, c_ctx,
           w_mod, b_mod, norm_g,
           conv_w1, conv_b1, conv_wdw, conv_bdw, conv_ln_g, conv_ln_b, conv_w2, conv_b2,
           gqa_wq, gqa_wk, gqa_wv, gqa_gq, gqa_gk, gqa_wo,
           mla_wdq, mla_gq, mla_wuq, mla_wdkv, mla_gkv, mla_wuk, mla_wuv, mla_wo,
           ffn_wg, ffn_wu, ffn_wd,
           moe_wr, moe_br, moe_wg, moe_wu, moe_wd):
    b, s, d = x_prompt.shape
    db, t, _ = x_sample.shape
    geo = _Geom(b, s, db, t)
    depth = w_mod.shape[0]
    past = cache_gqa_k.shape[2]
    hd = gqa_gq.shape[-1]
    bf = lambda a: a.astype(BF16)

    x = jnp.concatenate([x_prompt.reshape(geo.P, d), x_sample.reshape(db * t, d)], axis=0)

    n_cond = 1 + db
    cond = jnp.concatenate([c_ctx[None, :], c], axis=0)
    cond = jnp.pad(cond, ((0, -n_cond % 8), (0, 0)))
    mods = _mod_call(cond, w_mod, b_mod).reshape(depth, cond.shape[0], N_MOD, d)

    rope_tm = 512
    cos_hd, sin_hd = _rope_table(t, hd, rope_tm)
    cos_pe, sin_pe = _rope_table(t, ROPE_DIM, rope_tm)

    ks, vs, cs, ps = [], [], [], []
    for i in range(depth):
        mod, ng = mods[i], norm_g[i]
        j = i // N_MIXERS
        if i % N_MIXERS == 0:
            u = _conv_in_call(geo, x, mod, ng, bf(conv_w1[j]), conv_b1[j])
            x = _conv_out_call(geo, u, x, mod, ng, conv_wdw[j], conv_bdw[j], conv_ln_g[j],
                               conv_ln_b[j], bf(conv_w2[j]), conv_b2[j])
        elif i % N_MIXERS == 1:
            wqkv = bf(jnp.concatenate([gqa_wq[j], gqa_wk[j], gqa_wv[j]], axis=1))
            q, k, v = _qkv_call(geo, x, mod, ng, wqkv, gqa_gq[j], gqa_gk[j], cos_hd, sin_hd)
            ks.append(k[:geo.P].reshape(b, s, N_KV_HEADS, hd))
            vs.append(v[:geo.P].reshape(b, s, N_KV_HEADS, hd))
            kw = dict(n_kv=N_KV_HEADS, n_group=N_HEADS // N_KV_HEADS, dk=hd, dv=hd)
            o_p = _attn_call(q, k, v, None, None, n_batch=b, seq=s, row0=0, tq=min(s, 256),
                             name="gqa_attn_prompt", **kw)
            o_s = _attn_call(q, k, v,
                             cache_gqa_k[:, j].reshape(db, past, N_KV_HEADS * hd),
                             cache_gqa_v[:, j].reshape(db, past, N_KV_HEADS * hd),
                             n_batch=db, seq=t, row0=geo.P, tq=256, name="gqa_attn_sample", **kw)
            x = _attn_out_call(geo, x, o_p, o_s, mod, ng, bf(gqa_wo[j]))
        else:
            qr, kr = mla_gq.shape[-1], mla_gkv.shape[-1]
            wd_cat = bf(jnp.pad(jnp.concatenate([mla_wdq[j], mla_wdkv[j]], axis=1),
                                ((0, 0), (0, LANES - ROPE_DIM))))
            wuq_pad = bf(jnp.pad(mla_wuq[j], ((0, 0), (0, 0), (0, MLA_QK - NOPE_DIM - ROPE_DIM)))
                         ).reshape(qr, MLA_HEADS * MLA_QK)
            wuk = bf(mla_wuk[j]).reshape(kr, MLA_HEADS * NOPE_DIM)
            wuv = bf(mla_wuv[j]).reshape(kr, MLA_HEADS * V_DIM)
            q, kcat, v, ckv, kpe = _mla_in_call(geo, x, mod, ng, wd_cat, mla_gq[j], mla_gkv[j],
                                                wuq_pad, wuk, wuv, cos_pe, sin_pe)
            cs.append(ckv[:geo.P].reshape(b, s, kr))
            ps.append(kpe[:geo.P].reshape(b, s, ROPE_DIM))
            kc, vc = _mla_cache_call(
                cache_mla_ckv[:, j].reshape(db * past, kr),
                jnp.pad(cache_mla_kpe[:, j].reshape(db * past, ROPE_DIM), ((0, 0), (0, LANES - ROPE_DIM))),
                wuk, wuv)
            kw = dict(n_kv=MLA_HEADS, n_group=1, dk=MLA_QK, dv=V_DIM)
            o_p = _attn_call(q, kcat, v, None, None, n_batch=b, seq=s, row0=0, tq=min(s, 256),
                             name="mla_attn_prompt", **kw)
            o_s = _attn_call(q, kcat, v, kc.reshape(db, past, -1), vc.reshape(db, past, -1),
                             n_batch=db, seq=t, row0=geo.P, tq=512, name="mla_attn_sample", **kw)
            x = _attn_out_call(geo, x, o_p, o_s, mod, ng, bf(mla_wo[j]))
        f = i // 2
        if i % 2 == 0:
            x = _ffn_call(geo, x, mod, ng, bf(ffn_wg[f]), bf(ffn_wu[f]), bf(ffn_wd[f]))
        else:
            wr_pad = jnp.pad(moe_wr[f], ((0, 0), (0, LANES - N_EXPERTS)))
            br_pad = jnp.pad(moe_br[f], (0, LANES - N_EXPERTS)).reshape(1, LANES)
            h, topw, topi = _route_call(geo, x, mod, ng, wr_pad, br_pad)
            src, gate, tile_expert, n_used, dest = _dispatch(topi, topw, MOE_TR)
            y = _moe_call(h, src, gate, tile_expert, n_used,
                          bf(moe_wg[f]), bf(moe_wu[f]), bf(moe_wd[f]), MOE_TR)
            x = _combine_call(geo, x, y, dest, mod, ng)

    y_prompt = x[:geo.P].reshape(b, s, d)
    y_sample = x[geo.P:].reshape(db, t, d)
    return (y_prompt, y_sample, jnp.stack(ks, axis=1), jnp.stack(vs, axis=1),
            jnp.stack(cs, axis=1), jnp.stack(ps, axis=1))
```

```python
import functools

import jax
import jax.numpy as jnp
from jax import lax
from jax.experimental import pallas as pl
from jax.experimental.pallas import tpu as pltpu

F32 = jnp.float32
BF16 = jnp.bfloat16

EPS = 1e-6
GRID_W = 64
ROPE_THETA = 10000.0
N_MIXERS = 3
N_MOD = 6
CONV_W = 31
CONV_PAD = CONV_W // 2
N_HEADS = 16
N_KV_HEADS = 4
MLA_HEADS = 16
NOPE_DIM = 128
ROPE_DIM = 64
V_DIM = 128
N_EXPERTS = 8
TOP_K = 2
MOE_TR = 512

LANES = 128
SUBLANES = 8
HALO = 16
MLA_QK = 256
VMEM_LIMIT = 56 * 1024 * 1024

SH1, SC1, G1, SH2, SC2, G2 = range(6)
NG_PRE_MIX, NG_POST_MIX, NG_PRE_FFN, NG_POST_FFN = range(4)


def _cparams(sem):
    return pltpu.CompilerParams(dimension_semantics=sem, vmem_limit_bytes=VMEM_LIMIT)


def _rms(x):
    return x * lax.rsqrt(jnp.mean(x * x, axis=-1, keepdims=True) + EPS)


def _normmod(x, g, sc, sh):
    return (_rms(x) * g) * (1.0 + sc) + sh


def _silu(x):
    return x * jax.nn.sigmoid(x)


def _dot(a, b):
    return jnp.dot(a, b, preferred_element_type=F32)


def _rope(x, cos, sin_signed, hb):
    lane = lax.broadcasted_iota(jnp.int32, x.shape, 1)
    first = (lane % (2 * hb)) < hb
    rot = jnp.where(first, pltpu.roll(x, LANES - hb, 1), pltpu.roll(x, hb, 1))
    return x * cos + rot * sin_signed


class _Geom:
    def __init__(self, b, s, db, t):
        self.B, self.S, self.DB, self.T = b, s, db, t
        self.P = b * s
        self.M = self.P + db * t

    def check_tile(self, tm):
        assert self.P % tm == 0 and self.T % tm == 0, (self.P, self.T, tm)

    def mod_row(self, i, tm):
        npt = self.P // tm
        return jnp.where(i < npt, 0, 1 + (i - npt) // (self.T // tm))

    def rope_blk(self, i, tm):
        npt = self.P // tm
        return jnp.where(i < npt, self.T // tm, (i - npt) % (self.T // tm))


def _mod_kernel(c_ref, w_ref, b_ref, o_ref):
    c = c_ref[...]
    o_ref[...] = _dot(_silu(c).astype(BF16), w_ref[...].astype(BF16)) + b_ref[...]


def _mod_call(cond, w_mod, b_mod):
    depth, d, n = w_mod.shape
    rows = cond.shape[0]
    tn = 1024
    return pl.pallas_call(
        _mod_kernel,
        grid=(depth, n // tn),
        in_specs=[
            pl.BlockSpec((rows, d), lambda l, j: (0, 0)),
            pl.BlockSpec((None, d, tn), lambda l, j: (l, 0, j)),
            pl.BlockSpec((None, 1, tn), lambda l, j: (l, 0, j)),
        ],
        out_specs=pl.BlockSpec((None, rows, tn), lambda l, j: (l, 0, j)),
        out_shape=jax.ShapeDtypeStruct((depth, rows, n), F32),
        compiler_params=_cparams(("arbitrary", "arbitrary")),
        name="mod",
    )(cond, w_mod, b_mod.reshape(depth, 1, n))


def _conv_in_kernel(x_ref, mod_ref, ng_ref, wa_ref, wg_ref, ba_ref, bg_ref, u_ref, h_scr):
    @pl.when(pl.program_id(1) == 0)
    def _():
        h = _normmod(x_ref[...], ng_ref[NG_PRE_MIX:NG_PRE_MIX + 1, :],
                     mod_ref[SC1:SC1 + 1, :], mod_ref[SH1:SH1 + 1, :])
        h_scr[...] = h.astype(BF16)

    h = h_scr[...]
    a = _dot(h, wa_ref[...]) + ba_ref[...]
    g = _dot(h, wg_ref[...]) + bg_ref[...]
    u_ref[...] = a * jax.nn.sigmoid(g)


def _conv_in_call(geo, x, mod, ng, w1, b1):
    m, d = x.shape
    tm, tn = 512, 1024
    geo.check_tile(tm)
    nj = d // tn
    b1 = b1.reshape(1, 2 * d)
    return pl.pallas_call(
        _conv_in_kernel,
        grid=(m // tm, nj),
        in_specs=[
            pl.BlockSpec((tm, d), lambda i, j: (i, 0)),
            pl.BlockSpec((None, N_MOD, d), lambda i, j: (geo.mod_row(i, tm), 0, 0)),
            pl.BlockSpec((4, d), lambda i, j: (0, 0)),
            pl.BlockSpec((d, tn), lambda i, j: (0, j)),
            pl.BlockSpec((d, tn), lambda i, j: (0, j + nj)),
            pl.BlockSpec((1, tn), lambda i, j: (0, j)),
            pl.BlockSpec((1, tn), lambda i, j: (0, j + nj)),
        ],
        out_specs=pl.BlockSpec((tm, tn), lambda i, j: (i, j)),
        out_shape=jax.ShapeDtypeStruct((m, d), F32),
        scratch_shapes=[pltpu.VMEM((tm, d), BF16)],
        compiler_params=_cparams(("arbitrary", "arbitrary")),
        name="conv_in",
    )(x, mod, ng, w1, w1, b1, b1)


CONV_TM = 256
CONV_RC = 32
CONV_CC = 256


def _conv_out_kernel(tiles_p, tps_p, tps_s,
                     u_ref, up_ref, un_ref, x_ref, mod_ref, ng_ref, wdw_ref, bdw_ref,
                     lng_ref, lnb_ref, w2_ref, b2_ref, o_ref, ext, cv, sh):
    i = pl.program_id(0)
    is_p = i < tiles_p
    j = jnp.where(is_p, i % tps_p, (i - tiles_p) % tps_s)
    n = jnp.where(is_p, tps_p, tps_s)
    tm, d = u_ref.shape
    ext[HALO:HALO + tm, :] = u_ref[...]
    ext[0:HALO, :] = jnp.where(j == 0, 0.0, up_ref[...])
    ext[HALO + tm:HALO + tm + HALO, :] = jnp.where(j == n - 1, 0.0, un_ref[...])

    def chunk(c, carry):
        c0 = pl.multiple_of(c * CONV_CC, CONV_CC)
        e = ext[:, pl.ds(c0, CONV_CC)]
        for p in range(1, SUBLANES):
            sh[p - 1] = pltpu.roll(e, e.shape[0] - p, 0)
        for r in range(0, tm, CONV_RC):
            acc = jnp.zeros((CONV_RC, CONV_CC), F32)
            for k in range(CONV_W):
                r0 = HALO - CONV_PAD + r + k
                p = r0 % SUBLANES
                a0 = r0 - p
                if p == 0:
                    tap = ext[a0:a0 + CONV_RC, pl.ds(c0, CONV_CC)]
                else:
                    tap = sh[p - 1, a0:a0 + CONV_RC, :]
                acc = acc + tap * wdw_ref[k:k + 1, pl.ds(c0, CONV_CC)]
            cv[r:r + CONV_RC, pl.ds(c0, CONV_CC)] = acc
        return carry

    lax.fori_loop(0, d // CONV_CC, chunk, 0)

    v = cv[...] + bdw_ref[...]
    mu = jnp.mean(v, axis=-1, keepdims=True)
    xc = v - mu
    var = jnp.mean(xc * xc, axis=-1, keepdims=True)
    y = _silu(xc * lax.rsqrt(var + EPS) * lng_ref[...] + lnb_ref[...])
    out = _dot(y.astype(BF16), w2_ref[...]) + b2_ref[...]
    o_ref[...] = x_ref[...] + mod_ref[G1:G1 + 1, :] * (_rms(out) * ng_ref[NG_POST_MIX:NG_POST_MIX + 1, :])


def _conv_out_call(geo, u, x, mod, ng, wdw, bdw, lng, lnb, w2, b2):
    m, d = x.shape
    tm = CONV_TM
    assert geo.S % tm == 0 and geo.T % tm == 0
    hb = tm // HALO
    nhb = m // HALO
    kern = functools.partial(_conv_out_kernel, geo.P // tm, geo.S // tm, geo.T // tm)
    vec = lambda a: a.reshape(1, d)
    return pl.pallas_call(
        kern,
        grid=(m // tm,),
        in_specs=[
            pl.BlockSpec((tm, d), lambda i: (i, 0)),
            pl.BlockSpec((HALO, d), lambda i: (jnp.maximum(i * hb - 1, 0), 0)),
            pl.BlockSpec((HALO, d), lambda i: (jnp.minimum((i + 1) * hb, nhb - 1), 0)),
            pl.BlockSpec((tm, d), lambda i: (i, 0)),
            pl.BlockSpec((None, N_MOD, d), lambda i: (geo.mod_row(i, tm), 0, 0)),
            pl.BlockSpec((4, d), lambda i: (0, 0)),
            pl.BlockSpec((CONV_W, d), lambda i: (0, 0)),
            pl.BlockSpec((1, d), lambda i: (0, 0)),
            pl.BlockSpec((1, d), lambda i: (0, 0)),
            pl.BlockSpec((1, d), lambda i: (0, 0)),
            pl.BlockSpec((d, d), lambda i: (0, 0)),
            pl.BlockSpec((1, d), lambda i: (0, 0)),
        ],
        out_specs=pl.BlockSpec((tm, d), lambda i: (i, 0)),
        out_shape=jax.ShapeDtypeStruct((m, d), F32),
        scratch_shapes=[pltpu.VMEM((tm + 2 * HALO, d), F32), pltpu.VMEM((tm, d), F32),
                        pltpu.VMEM((SUBLANES - 1, tm + 2 * HALO, CONV_CC), F32)],
        compiler_params=_cparams(("arbitrary",)),
        name="conv_out",
    )(u, u, u, x, mod, ng, wdw, vec(bdw), vec(lng), vec(lnb), w2, vec(b2))


def _ffn_kernel(x_ref, mod_ref, ng_ref, wg_ref, wu_ref, wd_ref, o_ref, h_scr, acc):
    f = pl.program_id(1)

    @pl.when(f == 0)
    def _():
        h = _normmod(x_ref[...], ng_ref[NG_PRE_FFN:NG_PRE_FFN + 1, :],
                     mod_ref[SC2:SC2 + 1, :], mod_ref[SH2:SH2 + 1, :])
        h_scr[...] = h.astype(BF16)
        acc[...] = jnp.zeros_like(acc)

    h = h_scr[...]
    t = _silu(_dot(h, wg_ref[...])) * _dot(h, wu_ref[...])
    acc[...] += _dot(t.astype(BF16), wd_ref[...])

    @pl.when(f == pl.num_programs(1) - 1)
    def _():
        o_ref[...] = x_ref[...] + mod_ref[G2:G2 + 1, :] * (
            _rms(acc[...]) * ng_ref[NG_POST_FFN:NG_POST_FFN + 1, :])


def _ffn_call(geo, x, mod, ng, wg, wu, wd):
    m, d = x.shape
    dff = wg.shape[1]
    tm, tf = 512, 512
    geo.check_tile(tm)
    return pl.pallas_call(
        _ffn_kernel,
        grid=(m // tm, dff // tf),
        in_specs=[
            pl.BlockSpec((tm, d), lambda i, f: (i, 0)),
            pl.BlockSpec((None, N_MOD, d), lambda i, f: (geo.mod_row(i, tm), 0, 0)),
            pl.BlockSpec((4, d), lambda i, f: (0, 0)),
            pl.BlockSpec((d, tf), lambda i, f: (0, f)),
            pl.BlockSpec((d, tf), lambda i, f: (0, f)),
            pl.BlockSpec((tf, d), lambda i, f: (f, 0)),
        ],
        out_specs=pl.BlockSpec((tm, d), lambda i, f: (i, 0)),
        out_shape=jax.ShapeDtypeStruct((m, d), F32),
        scratch_shapes=[pltpu.VMEM((tm, d), BF16), pltpu.VMEM((tm, d), F32)],
        compiler_params=_cparams(("arbitrary", "arbitrary")),
        name="ffn",
    )(x, mod, ng, wg, wu, wd)


def _qkv_kernel(hd, scale, x_ref, mod_ref, ng_ref, w_ref, gq_ref, gk_ref, cos_ref, sin_ref,
                q_ref, k_ref, v_ref):
    h = _normmod(x_ref[...], ng_ref[NG_PRE_MIX:NG_PRE_MIX + 1, :],
                 mod_ref[SC1:SC1 + 1, :], mod_ref[SH1:SH1 + 1, :])
    r = _dot(h.astype(BF16), w_ref[...])
    nq = q_ref.shape[1]
    nk = k_ref.shape[1]
    cos = cos_ref[...]
    sin = sin_ref[...]
    for c in range(0, nq, hd):
        qh = _rope(_rms(r[:, c:c + hd]) * gq_ref[...], cos, sin, hd // 4)
        q_ref[:, c:c + hd] = (qh * scale).astype(BF16)
    for c in range(0, nk, hd):
        k_ref[:, c:c + hd] = _rope(_rms(r[:, nq + c:nq + c + hd]) * gk_ref[...], cos, sin, hd // 4)
    v_ref[...] = r[:, nq + nk:]


def _qkv_call(geo, x, mod, ng, wqkv, gq, gk, cos, sin):
    m, d = x.shape
    hd = gq.shape[-1]
    assert hd == LANES
    nq, nk = N_HEADS * hd, N_KV_HEADS * hd
    tm = 512
    geo.check_tile(tm)
    kern = functools.partial(_qkv_kernel, hd, hd ** -0.5)
    return pl.pallas_call(
        kern,
        grid=(m // tm,),
        in_specs=[
            pl.BlockSpec((tm, d), lambda i: (i, 0)),
            pl.BlockSpec((None, N_MOD, d), lambda i: (geo.mod_row(i, tm), 0, 0)),
            pl.BlockSpec((4, d), lambda i: (0, 0)),
            pl.BlockSpec((d, nq + 2 * nk), lambda i: (0, 0)),
            pl.BlockSpec((1, hd), lambda i: (0, 0)),
            pl.BlockSpec((1, hd), lambda i: (0, 0)),
            pl.BlockSpec((tm, LANES), lambda i: (geo.rope_blk(i, tm), 0)),
            pl.BlockSpec((tm, LANES), lambda i: (geo.rope_blk(i, tm), 0)),
        ],
        out_specs=[
            pl.BlockSpec((tm, nq), lambda i: (i, 0)),
            pl.BlockSpec((tm, nk), lambda i: (i, 0)),
            pl.BlockSpec((tm, nk), lambda i: (i, 0)),
        ],
        out_shape=[
            jax.ShapeDtypeStruct((m, nq), BF16),
            jax.ShapeDtypeStruct((m, nk), F32),
            jax.ShapeDtypeStruct((m, nk), F32),
        ],
        compiler_params=_cparams(("arbitrary",)),
        name="qkv",
    )(x, mod, ng, wqkv, gq.reshape(1, hd), gk.reshape(1, hd), cos, sin)


def _attn_kernel(n_group, dk, dv, has_cache, *refs):
    if has_cache:
        q_ref, kn_ref, vn_ref, kc_ref, vc_ref, o_ref, kn_s, vn_s, kc_s, vc_s = refs
    else:
        q_ref, kn_ref, vn_ref, o_ref, kn_s, vn_s = refs

    @pl.when(pl.program_id(2) == 0)
    def _():
        kn_s[...] = kn_ref[...].astype(BF16)
        vn_s[...] = vn_ref[...].astype(BF16)
        if has_cache:
            kc_s[...] = kc_ref[...].astype(BF16)
            vc_s[...] = vc_ref[...].astype(BF16)

    nt = (((1,), (1,)), ((), ()))
    for g in range(n_group):
        q = q_ref[:, g * dk:(g + 1) * dk]
        s_n = lax.dot_general(q, kn_s[...], nt, preferred_element_type=F32)
        mx = jnp.max(s_n, axis=-1, keepdims=True)
        if has_cache:
            s_c = lax.dot_general(q, kc_s[...], nt, preferred_element_type=F32)
            mx = jnp.maximum(mx, jnp.max(s_c, axis=-1, keepdims=True))
        p_n = jnp.exp(s_n - mx)
        den = jnp.sum(p_n, axis=-1, keepdims=True)
        o = _dot(p_n.astype(BF16), vn_s[...])
        if has_cache:
            p_c = jnp.exp(s_c - mx)
            den = den + jnp.sum(p_c, axis=-1, keepdims=True)
            o = o + _dot(p_c.astype(BF16), vc_s[...])
        o_ref[:, g * dv:(g + 1) * dv] = (o / den).astype(BF16)


def _attn_call(q, kn, vn, kc, vc, *, n_batch, seq, row0, n_kv, n_group, dk, dv, tq, name):
    has_cache = kc is not None
    assert seq % tq == 0 and row0 % seq == 0
    nqt = seq // tq
    qb0, kb0 = row0 // tq, row0 // seq
    in_specs = [
        pl.BlockSpec((tq, n_group * dk), lambda b, h, t: (qb0 + b * nqt + t, h)),
        pl.BlockSpec((seq, dk), lambda b, h, t: (kb0 + b, h)),
        pl.BlockSpec((seq, dv), lambda b, h, t: (kb0 + b, h)),
    ]
    scratch = [pltpu.VMEM((seq, dk), BF16), pltpu.VMEM((seq, dv), BF16)]
    args = [q, kn, vn]
    if has_cache:
        past = kc.shape[1]
        in_specs += [
            pl.BlockSpec((None, past, dk), lambda b, h, t: (b, 0, h)),
            pl.BlockSpec((None, past, dv), lambda b, h, t: (b, 0, h)),
        ]
        scratch += [pltpu.VMEM((past, dk), BF16), pltpu.VMEM((past, dv), BF16)]
        args += [kc, vc]
    kern = functools.partial(_attn_kernel, n_group, dk, dv, has_cache)
    return pl.pallas_call(
        kern,
        grid=(n_batch, n_kv, nqt),
        in_specs=in_specs,
        out_specs=pl.BlockSpec((tq, n_group * dv), lambda b, h, t: (b * nqt + t, h)),
        out_shape=jax.ShapeDtypeStruct((n_batch * seq, n_kv * n_group * dv), BF16),
        scratch_shapes=scratch,
        compiler_params=_cparams(("arbitrary", "arbitrary", "arbitrary")),
        name=name,
    )(*args)


def _attn_out_kernel(npt, x_ref, op_ref, os_ref, mod_ref, ng_ref, wo_ref, y_ref):
    def project(o_ref):
        out = _dot(o_ref[...], wo_ref[...])
        y_ref[...] = x_ref[...] + mod_ref[G1:G1 + 1, :] * (
            _rms(out) * ng_ref[NG_POST_MIX:NG_POST_MIX + 1, :])

    i = pl.program_id(0)
    pl.when(i < npt)(lambda: project(op_ref))
    pl.when(i >= npt)(lambda: project(os_ref))


def _attn_out_call(geo, x, o_p, o_s, mod, ng, wo):
    m, d = x.shape
    tm = 512
    geo.check_tile(tm)
    npt = geo.P // tm
    n = o_p.shape[1]
    return pl.pallas_call(
        functools.partial(_attn_out_kernel, npt),
        grid=(m // tm,),
        in_specs=[
            pl.BlockSpec((tm, d), lambda i: (i, 0)),
            pl.BlockSpec((tm, n), lambda i: (jnp.minimum(i, npt - 1), 0)),
            pl.BlockSpec((tm, n), lambda i: (jnp.maximum(i - npt, 0), 0)),
            pl.BlockSpec((None, N_MOD, d), lambda i: (geo.mod_row(i, tm), 0, 0)),
            pl.BlockSpec((4, d), lambda i: (0, 0)),
            pl.BlockSpec(wo.shape, lambda i: (0, 0)),
        ],
        out_specs=pl.BlockSpec((tm, d), lambda i: (i, 0)),
        out_shape=jax.ShapeDtypeStruct((m, d), F32),
        compiler_params=_cparams(("arbitrary",)),
        name="attn_out",
    )(x, o_p, o_s, mod, ng, wo)


HI_MASK = 0xFFFF0000


def _as_bf16_bits(v):
    return lax.bitcast_convert_type(v.astype(BF16).astype(F32), jnp.uint32)


def _pack_rows(v, out_ref):
    n, d = v.shape
    assert d == 2 * SUBLANES * LANES
    for s in range(SUBLANES):
        lo = _as_bf16_bits(v[:, s * LANES:(s + 1) * LANES])
        hi = _as_bf16_bits(v[:, (s + SUBLANES) * LANES:(s + SUBLANES + 1) * LANES])
        out_ref[pl.ds(s, n, stride=SUBLANES), :] = (lo >> 16) | (hi & jnp.uint32(HI_MASK))


def _unpack_rows(buf, n, s):
    w = buf[pl.ds(s, n, stride=SUBLANES), :]
    lo = lax.bitcast_convert_type(w << 16, F32)
    hi = lax.bitcast_convert_type(w & jnp.uint32(HI_MASK), F32)
    return lo, hi


def _tile_copy(src_hbm, src_row, dst, dst_row, sem):
    return pltpu.make_async_copy(
        src_hbm.at[pl.ds(pl.multiple_of(src_row * SUBLANES, SUBLANES), SUBLANES), :],
        dst.at[pl.ds(pl.multiple_of(dst_row * SUBLANES, SUBLANES), SUBLANES), :], sem)


def _start_gather(idx_ref, src_hbm, dst, n, sem):
    def start(r, c):
        _tile_copy(src_hbm, idx_ref[0, r], dst, r, sem).start()
        return c

    lax.fori_loop(0, n, start, 0, unroll=8)


def _wait_gather(src_hbm, dst, n, sem):
    pltpu.make_async_copy(src_hbm.at[pl.ds(0, n * SUBLANES), :], dst, sem).wait()


def _pipelined_gather(i, n_steps, idx_ref, nxt_ref, src_hbm, bufs, n, sems, consume):
    for sl in (0, 1):
        @pl.when(i % 2 == sl)
        def _(sl=sl):
            if sl == 0:
                @pl.when(i == 0)
                def _():
                    _start_gather(idx_ref, src_hbm, bufs[0], n, sems.at[0])

            @pl.when(i + 1 < n_steps)
            def _():
                _start_gather(nxt_ref, src_hbm, bufs[1 - sl], n, sems.at[1 - sl])

            _wait_gather(src_hbm, bufs[sl], n, sems.at[sl])
            consume(bufs[sl])


def _route_kernel(n_exp, x_ref, mod_ref, ng_ref, wr_ref, br_ref, h_ref, topw_ref, topi_ref):
    h = _normmod(x_ref[...], ng_ref[NG_PRE_FFN:NG_PRE_FFN + 1, :],
                 mod_ref[SC2:SC2 + 1, :], mod_ref[SH2:SH2 + 1, :])
    _pack_rows(h, h_ref)
    logits = jnp.dot(h, wr_ref[...], preferred_element_type=F32,
                     precision=lax.Precision.HIGHEST) + br_ref[...]
    lane = lax.broadcasted_iota(jnp.int32, logits.shape, 1)
    neg = jnp.float32(-jnp.inf)
    logits = jnp.where(lane < n_exp, logits, neg)
    m1 = jnp.max(logits, axis=-1, keepdims=True)
    i1 = jnp.min(jnp.where(logits == m1, lane, LANES), axis=-1, keepdims=True)
    rest = jnp.where(lane == i1, neg, logits)
    m2 = jnp.max(rest, axis=-1, keepdims=True)
    i2 = jnp.min(jnp.where(rest == m2, lane, LANES), axis=-1, keepdims=True)
    e = jnp.exp(m2 - m1)
    w1 = 1.0 / (1.0 + e)
    w2 = e / (1.0 + e)
    topw_ref[...] = jnp.where(lane == 0, w1, jnp.where(lane == 1, w2, 0.0))
    topi_ref[...] = jnp.where(lane == 0, i1, jnp.where(lane == 1, i2, 0))


def _route_call(geo, x, mod, ng, wr_pad, br_pad):
    m, d = x.shape
    tm = 512
    geo.check_tile(tm)
    kern = functools.partial(_route_kernel, N_EXPERTS)
    return pl.pallas_call(
        kern,
        grid=(m // tm,),
        in_specs=[
            pl.BlockSpec((tm, d), lambda i: (i, 0)),
            pl.BlockSpec((None, N_MOD, d), lambda i: (geo.mod_row(i, tm), 0, 0)),
            pl.BlockSpec((4, d), lambda i: (0, 0)),
            pl.BlockSpec((d, LANES), lambda i: (0, 0)),
            pl.BlockSpec((1, LANES), lambda i: (0, 0)),
        ],
        out_specs=[
            pl.BlockSpec((tm * SUBLANES, LANES), lambda i: (i, 0)),
            pl.BlockSpec((tm, LANES), lambda i: (i, 0)),
            pl.BlockSpec((tm, LANES), lambda i: (i, 0)),
        ],
        out_shape=[
            jax.ShapeDtypeStruct((m * SUBLANES, LANES), jnp.uint32),
            jax.ShapeDtypeStruct((m, LANES), F32),
            jax.ShapeDtypeStruct((m, LANES), jnp.int32),
        ],
        compiler_params=_cparams(("arbitrary",)),
        name="route",
    )(x, mod, ng, wr_pad, br_pad)


def _dispatch(topi, topw, tr):
    m = topi.shape[0]
    n_asg = TOP_K * m
    n_tiles = n_asg // tr + N_EXPERTS
    e = topi[:, :TOP_K].reshape(n_asg)
    w = topw[:, :TOP_K].reshape(n_asg)
    onehot = (e[:, None] == jnp.arange(N_EXPERTS, dtype=jnp.int32)[None, :]).astype(jnp.int32)
    csum = jnp.cumsum(onehot, axis=0)
    count = csum[-1]
    rank = jnp.sum(csum * onehot, axis=1) - 1
    ptiles = (count + tr - 1) // tr
    tile_end = jnp.cumsum(ptiles)
    tile_start = tile_end - ptiles
    n_used = tile_end[-1:]
    dest = tile_start[e] * tr + rank
    asg = jnp.full((n_tiles * tr,), -1, jnp.int32).at[dest].set(jnp.arange(n_asg, dtype=jnp.int32))
    src = jnp.maximum(asg, 0) // TOP_K
    gate = jnp.where(asg >= 0, w[jnp.maximum(asg, 0)], 0.0)
    tile_expert = jnp.minimum(
        jnp.sum((jnp.arange(n_tiles, dtype=jnp.int32)[:, None] >= tile_end[None, :]).astype(jnp.int32), axis=1),
        N_EXPERTS - 1)
    return src, gate, tile_expert, n_used.astype(jnp.int32), dest


def _moe_kernel(te_ref, nu_ref, src_ref, nxt_ref, h_hbm, g_ref, wg_ref, wu_ref, wd_ref, y_ref,
                buf0, buf1, hbf, acc, sems):
    i = pl.program_id(0)
    f = pl.program_id(1)
    n_used = nu_ref[0]
    used = i < n_used
    tr = hbf.shape[0]

    def unpack(buf):
        for s in range(SUBLANES):
            lo, hi = _unpack_rows(buf, tr, s)
            hbf[:, s * LANES:(s + 1) * LANES] = lo.astype(BF16)
            hbf[:, (s + SUBLANES) * LANES:(s + SUBLANES + 1) * LANES] = hi.astype(BF16)

    @pl.when(used & (f == 0))
    def _():
        _pipelined_gather(i, n_used, src_ref, nxt_ref, h_hbm, (buf0, buf1), tr, sems, unpack)
        acc[...] = jnp.zeros_like(acc)

    @pl.when(used)
    def _():
        h = hbf[...]
        t = _silu(_dot(h, wg_ref[...])) * _dot(h, wu_ref[...]) * g_ref[...]
        acc[...] += _dot(t.astype(BF16), wd_ref[...])

    last = f == pl.num_programs(1) - 1

    @pl.when(used & last)
    def _():
        _pack_rows(acc[...], y_ref)

    @pl.when(jnp.logical_not(used) & last)
    def _():
        y_ref[...] = jnp.zeros_like(y_ref)


def _moe_call(h, src, gate, tile_expert, n_used, wg, wu, wd, tr):
    n_exp, d, dfe = wg.shape
    n_tiles = tile_expert.shape[0]
    tf = 256
    assert dfe % tf == 0
    nf = dfe // tf

    def fblk(i, f, nu):
        return jnp.where(i < nu[0], f, nf - 1)

    src = src.reshape(n_tiles, 1, tr)
    tile_rows = tr * SUBLANES
    grid_spec = pltpu.PrefetchScalarGridSpec(
        num_scalar_prefetch=2,
        grid=(n_tiles, nf),
        in_specs=[
            pl.BlockSpec((None, 1, tr), lambda i, f, te, nu: (i, 0, 0), memory_space=pltpu.SMEM),
            pl.BlockSpec((None, 1, tr), lambda i, f, te, nu: (jnp.minimum(i + 1, n_tiles - 1), 0, 0),
                         memory_space=pltpu.SMEM),
            pl.BlockSpec(memory_space=pl.ANY),
            pl.BlockSpec((tr, 1), lambda i, f, te, nu: (i, 0)),
            pl.BlockSpec((None, d, tf), lambda i, f, te, nu: (te[i], 0, fblk(i, f, nu))),
            pl.BlockSpec((None, d, tf), lambda i, f, te, nu: (te[i], 0, fblk(i, f, nu))),
            pl.BlockSpec((None, tf, d), lambda i, f, te, nu: (te[i], fblk(i, f, nu), 0)),
        ],
        out_specs=pl.BlockSpec((tile_rows, LANES), lambda i, f, te, nu: (i, 0)),
        scratch_shapes=[pltpu.VMEM((tile_rows, LANES), jnp.uint32), pltpu.VMEM((tile_rows, LANES), jnp.uint32),
                        pltpu.VMEM((tr, d), BF16), pltpu.VMEM((tr, d), F32),
                        pltpu.SemaphoreType.DMA((2,))],
    )
    return pl.pallas_call(
        _moe_kernel,
        grid_spec=grid_spec,
        out_shape=jax.ShapeDtypeStruct((n_tiles * tile_rows, LANES), jnp.uint32),
        compiler_params=_cparams(("arbitrary", "arbitrary")),
        name="moe",
    )(tile_expert, n_used, src, src, h, gate.reshape(n_tiles * tr, 1), wg, wu, wd)


def _combine_kernel(npt, pos_ref, nxt_ref, y_hbm, x_ref, mod_ref, ng_ref, *refs):
    outs, (buf0, buf1, ysum, sems) = refs[:-4], refs[-4:]
    i = pl.program_id(0)
    tm = x_ref.shape[0]

    def add_pairs(buf):
        for s in range(SUBLANES):
            lo, hi = _unpack_rows(buf, TOP_K * tm, s)
            ysum[:, s * LANES:(s + 1) * LANES] = lo[:tm] + lo[tm:]
            ysum[:, (s + SUBLANES) * LANES:(s + SUBLANES + 1) * LANES] = hi[:tm] + hi[tm:]

    _pipelined_gather(i, pl.num_programs(0), pos_ref, nxt_ref, y_hbm, (buf0, buf1), TOP_K * tm, sems,
                      add_pairs)
    res = x_ref[...] + mod_ref[G2:G2 + 1, :] * (
        _rms(ysum[...]) * ng_ref[NG_POST_FFN:NG_POST_FFN + 1, :])
    if npt is None:
        outs[0][...] = res
    else:
        @pl.when(i < npt)
        def _():
            outs[0][...] = res

        @pl.when(i >= npt)
        def _():
            outs[1][...] = res


def _combine_call(geo, x, y, dest, mod, ng, split):
    m, d = x.shape
    tm = 256
    geo.check_tile(tm)
    nt = m // tm
    npt = geo.P // tm
    pos = dest.reshape(nt, tm, TOP_K).transpose(0, 2, 1).reshape(nt, 1, TOP_K * tm)
    if split:
        out_specs = [pl.BlockSpec((tm, d), lambda i: (jnp.minimum(i, npt - 1), 0)),
                     pl.BlockSpec((tm, d), lambda i: (jnp.maximum(i - npt, 0), 0))]
        out_shape = [jax.ShapeDtypeStruct((geo.P, d), F32), jax.ShapeDtypeStruct((m - geo.P, d), F32)]
    else:
        out_specs = pl.BlockSpec((tm, d), lambda i: (i, 0))
        out_shape = jax.ShapeDtypeStruct((m, d), F32)
    buf = pltpu.VMEM((TOP_K * tm * SUBLANES, LANES), jnp.uint32)
    return pl.pallas_call(
        functools.partial(_combine_kernel, npt if split else None),
        grid=(nt,),
        in_specs=[
            pl.BlockSpec((None, 1, TOP_K * tm), lambda i: (i, 0, 0), memory_space=pltpu.SMEM),
            pl.BlockSpec((None, 1, TOP_K * tm), lambda i: (jnp.minimum(i + 1, nt - 1), 0, 0),
                         memory_space=pltpu.SMEM),
            pl.BlockSpec(memory_space=pl.ANY),
            pl.BlockSpec((tm, d), lambda i: (i, 0)),
            pl.BlockSpec((None, N_MOD, d), lambda i: (geo.mod_row(i, tm), 0, 0)),
            pl.BlockSpec((4, d), lambda i: (0, 0)),
        ],
        out_specs=out_specs,
        out_shape=out_shape,
        scratch_shapes=[buf, buf, pltpu.VMEM((tm, d), F32), pltpu.SemaphoreType.DMA((2,))],
        compiler_params=_cparams(("arbitrary",)),
        name="moe_combine",
    )(pos, pos, y, x, mod, ng)


def _mla_in_kernel(scale, x_ref, mod_ref, ng_ref, wd_ref, gq_ref, gkv_ref, wuq_ref, wuk_ref, wuv_ref,
                   cos_ref, sin_ref, q_ref, kcat_ref, v_ref, ckv_ref, kpe_ref):
    h = _normmod(x_ref[...], ng_ref[NG_PRE_MIX:NG_PRE_MIX + 1, :],
                 mod_ref[SC1:SC1 + 1, :], mod_ref[SH1:SH1 + 1, :])
    r = _dot(h.astype(BF16), wd_ref[...])
    qr = gq_ref.shape[1]
    kr = gkv_ref.shape[1]
    rope_dim = kpe_ref.shape[1]
    cos = cos_ref[...]
    sin = sin_ref[...]
    cq = (_rms(r[:, :qr]) * gq_ref[...]).astype(BF16)
    ckv = _rms(r[:, qr:qr + kr]) * gkv_ref[...]
    kpe = r[:, qr + kr:qr + kr + LANES]
    ckv_ref[...] = ckv
    kpe_ref[...] = kpe[:, :rope_dim]
    ckv_b = ckv.astype(BF16)
    q = _dot(cq, wuq_ref[...])
    kn = _dot(ckv_b, wuk_ref[...])
    v_ref[...] = _dot(ckv_b, wuv_ref[...]).astype(BF16)
    kpe_r = _rope(kpe, cos, sin, rope_dim // 4).astype(BF16)
    for hh in range(q_ref.shape[1] // MLA_QK):
        c = hh * MLA_QK
        q_ref[:, c:c + NOPE_DIM] = (q[:, c:c + NOPE_DIM] * scale).astype(BF16)
        q_ref[:, c + NOPE_DIM:c + MLA_QK] = (
            _rope(q[:, c + NOPE_DIM:c + MLA_QK], cos, sin, rope_dim // 4) * scale).astype(BF16)
        kcat_ref[:, c:c + NOPE_DIM] = kn[:, hh * NOPE_DIM:(hh + 1) * NOPE_DIM].astype(BF16)
        kcat_ref[:, c + NOPE_DIM:c + MLA_QK] = kpe_r


def _mla_in_call(geo, x, mod, ng, wd_cat, gq, gkv, wuq_pad, wuk, wuv, cos, sin):
    m, d = x.shape
    qr, kr = gq.shape[-1], gkv.shape[-1]
    nqk = wuq_pad.shape[1]
    nv = wuv.shape[1]
    tm = 512
    geo.check_tile(tm)
    kern = functools.partial(_mla_in_kernel, (NOPE_DIM + ROPE_DIM) ** -0.5)
    full = lambda a: pl.BlockSpec(a.shape, lambda i: (0, 0))
    row = lambda n: pl.BlockSpec((tm, n), lambda i: (i, 0))
    return pl.pallas_call(
        kern,
        grid=(m // tm,),
        in_specs=[
            row(d),
            pl.BlockSpec((None, N_MOD, d), lambda i: (geo.mod_row(i, tm), 0, 0)),
            pl.BlockSpec((4, d), lambda i: (0, 0)),
            full(wd_cat),
            pl.BlockSpec((1, qr), lambda i: (0, 0)),
            pl.BlockSpec((1, kr), lambda i: (0, 0)),
            full(wuq_pad), full(wuk), full(wuv),
            pl.BlockSpec((tm, LANES), lambda i: (geo.rope_blk(i, tm), 0)),
            pl.BlockSpec((tm, LANES), lambda i: (geo.rope_blk(i, tm), 0)),
        ],
        out_specs=[row(nqk), row(nqk), row(nv), row(kr), row(ROPE_DIM)],
        out_shape=[
            jax.ShapeDtypeStruct((m, nqk), BF16),
            jax.ShapeDtypeStruct((m, nqk), BF16),
            jax.ShapeDtypeStruct((m, nv), BF16),
            jax.ShapeDtypeStruct((m, kr), F32),
            jax.ShapeDtypeStruct((m, ROPE_DIM), F32),
        ],
        compiler_params=_cparams(("arbitrary",)),
        name="mla_in",
    )(x, mod, ng, wd_cat, gq.reshape(1, qr), gkv.reshape(1, kr), wuq_pad, wuk, wuv, cos, sin)


def _mla_cache_kernel(ckv_ref, kpe_ref, wuk_ref, wuv_ref, kcat_ref, v_ref):
    ckv = ckv_ref[...].astype(BF16)
    kn = _dot(ckv, wuk_ref[...])
    v_ref[...] = _dot(ckv, wuv_ref[...]).astype(BF16)
    kpe = kpe_ref[...].astype(BF16)
    for hh in range(kcat_ref.shape[1] // MLA_QK):
        c = hh * MLA_QK
        kcat_ref[:, c:c + NOPE_DIM] = kn[:, hh * NOPE_DIM:(hh + 1) * NOPE_DIM].astype(BF16)
        kcat_ref[:, c + NOPE_DIM:c + MLA_QK] = kpe


def _mla_cache_call(ckv, kpe_pad, wuk, wuv):
    n, kr = ckv.shape
    tm = 512
    assert n % tm == 0
    nqk = MLA_HEADS * MLA_QK
    nv = wuv.shape[1]
    return pl.pallas_call(
        _mla_cache_kernel,
        grid=(n // tm,),
        in_specs=[
            pl.BlockSpec((tm, kr), lambda i: (i, 0)),
            pl.BlockSpec((tm, LANES), lambda i: (i, 0)),
            pl.BlockSpec(wuk.shape, lambda i: (0, 0)),
            pl.BlockSpec(wuv.shape, lambda i: (0, 0)),
        ],
        out_specs=[pl.BlockSpec((tm, nqk), lambda i: (i, 0)), pl.BlockSpec((tm, nv), lambda i: (i, 0))],
        out_shape=[jax.ShapeDtypeStruct((n, nqk), BF16), jax.ShapeDtypeStruct((n, nv), BF16)],
        compiler_params=_cparams(("arbitrary",)),
        name="mla_cache",
    )(ckv, kpe_pad, wuk, wuv)


def _rope_table(n_tokens, dim, n_ident):
    rows = n_tokens // GRID_W
    row = jnp.repeat(jnp.arange(rows, dtype=F32), GRID_W)
    col = jnp.tile(jnp.arange(GRID_W, dtype=F32), rows)
    half = dim // 2
    inv = ROPE_THETA ** (-jnp.arange(0, half, 2, dtype=F32) / half)
    ar = row[:, None] * inv[None, :]
    ac = col[:, None] * inv[None, :]
    ang = jnp.concatenate([ar, ar, ac, ac], axis=-1)
    cos = jnp.cos(ang)
    sin = jnp.sin(ang)
    hb = dim // 4
    first = (jnp.arange(dim) % (2 * hb)) < hb
    sin = jnp.where(first[None, :], -sin, sin)
    cos = jnp.pad(cos, ((0, n_ident), (0, LANES - dim)), constant_values=1.0)
    sin = jnp.pad(sin, ((0, n_ident), (0, LANES - dim)))
    return cos, sin


def kernel(x_prompt, x_sample, cache_gqa_k, cache_gqa_v, cache_mla_ckv, cache_mla_kpe, c, c_ctx,
           w_mod, b_mod, norm_g,
           conv_w1, conv_b1, conv_wdw, conv_bdw, conv_ln_g, conv_ln_b, conv_w2, conv_b2,
           gqa_wq, gqa_wk, gqa_wv, gqa_gq, gqa_gk, gqa_wo,
           mla_wdq, mla_gq, mla_wuq, mla_wdkv, mla_gkv, mla_wuk, mla_wuv, mla_wo,
           ffn_wg, ffn_wu, ffn_wd,
           moe_wr, moe_br, moe_wg, moe_wu, moe_wd):
    b, s, d = x_prompt.shape
    db, t, _ = x_sample.shape
    geo = _Geom(b, s, db, t)
    depth = w_mod.shape[0]
    past = cache_gqa_k.shape[2]
    hd = gqa_gq.shape[-1]
    bf = lambda a: a.astype(BF16)

    x = jnp.concatenate([x_prompt.reshape(geo.P, d), x_sample.reshape(db * t, d)], axis=0)

    n_cond = 1 + db
    cond = jnp.concatenate([c_ctx[None, :], c], axis=0)
    cond = jnp.pad(cond, ((0, -n_cond % 8), (0, 0)))
    mods = _mod_call(cond, w_mod, b_mod).reshape(depth, cond.shape[0], N_MOD, d)

    rope_tm = 512
    cos_hd, sin_hd = _rope_table(t, hd, rope_tm)
    cos_pe, sin_pe = _rope_table(t, ROPE_DIM, rope_tm)

    ks, vs, cs, ps = [], [], [], []
    for i in range(depth):
        mod, ng = mods[i], norm_g[i]
        j = i // N_MIXERS
        if i % N_MIXERS == 0:
            u = _conv_in_call(geo, x, mod, ng, bf(conv_w1[j]), conv_b1[j])
            x = _conv_out_call(geo, u, x, mod, ng, conv_wdw[j], conv_bdw[j], conv_ln_g[j],
                               conv_ln_b[j], bf(conv_w2[j]), conv_b2[j])
        elif i % N_MIXERS == 1:
            wqkv = bf(jnp.concatenate([gqa_wq[j], gqa_wk[j], gqa_wv[j]], axis=1))
            q, k, v = _qkv_call(geo, x, mod, ng, wqkv, gqa_gq[j], gqa_gk[j], cos_hd, sin_hd)
            ks.append(k[:geo.P].reshape(b, s, N_KV_HEADS, hd))
            vs.append(v[:geo.P].reshape(b, s, N_KV_HEADS, hd))
            kw = dict(n_kv=N_KV_HEADS, n_group=N_HEADS // N_KV_HEADS, dk=hd, dv=hd)
            o_p = _attn_call(q, k, v, None, None, n_batch=b, seq=s, row0=0, tq=min(s, 256),
                             name="gqa_attn_prompt", **kw)
            o_s = _attn_call(q, k, v,
                             cache_gqa_k[:, j].reshape(db, past, N_KV_HEADS * hd),
                             cache_gqa_v[:, j].reshape(db, past, N_KV_HEADS * hd),
                             n_batch=db, seq=t, row0=geo.P, tq=256, name="gqa_attn_sample", **kw)
            x = _attn_out_call(geo, x, o_p, o_s, mod, ng, bf(gqa_wo[j]))
        else:
            qr, kr = mla_gq.shape[-1], mla_gkv.shape[-1]
            wd_cat = bf(jnp.pad(jnp.concatenate([mla_wdq[j], mla_wdkv[j]], axis=1),
                                ((0, 0), (0, LANES - ROPE_DIM))))
            wuq_pad = bf(jnp.pad(mla_wuq[j], ((0, 0), (0, 0), (0, MLA_QK - NOPE_DIM - ROPE_DIM)))
                         ).reshape(qr, MLA_HEADS * MLA_QK)
            wuk = bf(mla_wuk[j]).reshape(kr, MLA_HEADS * NOPE_DIM)
            wuv = bf(mla_wuv[j]).reshape(kr, MLA_HEADS * V_DIM)
            q, kcat, v, ckv, kpe = _mla_in_call(geo, x, mod, ng, wd_cat, mla_gq[j], mla_gkv[j],
                                                wuq_pad, wuk, wuv, cos_pe, sin_pe)
            cs.append(ckv[:geo.P].reshape(b, s, kr))
            ps.append(kpe[:geo.P].reshape(b, s, ROPE_DIM))
            kc, vc = _mla_cache_call(
                cache_mla_ckv[:, j].reshape(db * past, kr),
                jnp.pad(cache_mla_kpe[:, j].reshape(db * past, ROPE_DIM), ((0, 0), (0, LANES - ROPE_DIM))),
                wuk, wuv)
            kw = dict(n_kv=MLA_HEADS, n_group=1, dk=MLA_QK, dv=V_DIM)
            o_p = _attn_call(q, kcat, v, None, None, n_batch=b, seq=s, row0=0, tq=min(s, 256),
                             name="mla_attn_prompt", **kw)
            o_s = _attn_call(q, kcat, v, kc.reshape(db, past, -1), vc.reshape(db, past, -1),
                             n_batch=db, seq=t, row0=geo.P, tq=512, name="mla_attn_sample", **kw)
            x = _attn_out_call(geo, x, o_p, o_s, mod, ng, bf(mla_wo[j]))
        f = i // 2
        if i % 2 == 0:
            x = _ffn_call(geo, x, mod, ng, bf(ffn_wg[f]), bf(ffn_wu[f]), bf(ffn_wd[f]))
        else:
            wr_pad = jnp.pad(moe_wr[f], ((0, 0), (0, LANES - N_EXPERTS)))
            br_pad = jnp.pad(moe_br[f], (0, LANES - N_EXPERTS)).reshape(1, LANES)
            h, topw, topi = _route_call(geo, x, mod, ng, wr_pad, br_pad)
            src, gate, tile_expert, n_used, dest = _dispatch(topi, topw, MOE_TR)
            y = _moe_call(h, src, gate, tile_expert, n_used,
                          bf(moe_wg[f]), bf(moe_wu[f]), bf(moe_wd[f]), MOE_TR)
            x = _combine_call(geo, x, y, dest, mod, ng, split=(i == depth - 1))

    y_prompt, y_sample = x if isinstance(x, (list, tuple)) else (x[:geo.P], x[geo.P:])
    return (y_prompt.reshape(b, s, d), y_sample.reshape(db, t, d), jnp.stack(ks, axis=1), jnp.stack(vs, axis=1),
            jnp.stack(cs, axis=1), jnp.stack(ps, axis=1))
```

```python
import functools

import jax
import jax.numpy as jnp
from jax import lax
from jax.experimental import pallas as pl
from jax.experimental.pallas import tpu as pltpu

F32 = jnp.float32
BF16 = jnp.bfloat16

EPS = 1e-6
GRID_W = 64
ROPE_THETA = 10000.0
N_MIXERS = 3
N_MOD = 6
CONV_W = 31
CONV_PAD = CONV_W // 2
N_HEADS = 16
N_KV_HEADS = 4
MLA_HEADS = 16
NOPE_DIM = 128
ROPE_DIM = 64
V_DIM = 128
N_EXPERTS = 8
TOP_K = 2
MOE_TR = 512

LANES = 128
SUBLANES = 8
HALO = 16
MLA_QK = 256
MLA_HPS = 4
VMEM_LIMIT = 56 * 1024 * 1024

SH1, SC1, G1, SH2, SC2, G2 = range(6)
NG_PRE_MIX, NG_POST_MIX, NG_PRE_FFN, NG_POST_FFN = range(4)


def _cparams(sem):
    return pltpu.CompilerParams(dimension_semantics=sem, vmem_limit_bytes=VMEM_LIMIT)


def _rms(x):
    return x * lax.rsqrt(jnp.mean(x * x, axis=-1, keepdims=True) + EPS)


def _normmod(x, g, sc, sh):
    return (_rms(x) * g) * (1.0 + sc) + sh


def _silu(x):
    return x * jax.nn.sigmoid(x)


def _dot(a, b):
    return jnp.dot(a, b, preferred_element_type=F32)


def _rope(x, cos, sin_signed, hb):
    lane = lax.broadcasted_iota(jnp.int32, x.shape, 1)
    first = (lane % (2 * hb)) < hb
    rot = jnp.where(first, pltpu.roll(x, LANES - hb, 1), pltpu.roll(x, hb, 1))
    return x * cos + rot * sin_signed


class _Geom:
    def __init__(self, b, s, db, t):
        self.B, self.S, self.DB, self.T = b, s, db, t
        self.P = b * s
        self.M = self.P + db * t

    def check_tile(self, tm):
        assert self.P % tm == 0 and self.T % tm == 0, (self.P, self.T, tm)

    def mod_row(self, i, tm):
        npt = self.P // tm
        return jnp.where(i < npt, 0, 1 + (i - npt) // (self.T // tm))

    def rope_blk(self, i, tm):
        npt = self.P // tm
        return jnp.where(i < npt, self.T // tm, (i - npt) % (self.T // tm))


def _mod_kernel(c_ref, w_ref, b_ref, o_ref):
    c = c_ref[...]
    o_ref[...] = _dot(_silu(c).astype(BF16), w_ref[...].astype(BF16)) + b_ref[...]


def _mod_call(cond, w_mod, b_mod):
    depth, d, n = w_mod.shape
    rows = cond.shape[0]
    tn = 1024
    return pl.pallas_call(
        _mod_kernel,
        grid=(depth, n // tn),
        in_specs=[
            pl.BlockSpec((rows, d), lambda l, j: (0, 0)),
            pl.BlockSpec((None, d, tn), lambda l, j: (l, 0, j)),
            pl.BlockSpec((None, 1, tn), lambda l, j: (l, 0, j)),
        ],
        out_specs=pl.BlockSpec((None, rows, tn), lambda l, j: (l, 0, j)),
        out_shape=jax.ShapeDtypeStruct((depth, rows, n), F32),
        compiler_params=_cparams(("arbitrary", "arbitrary")),
        name="mod",
    )(cond, w_mod, b_mod.reshape(depth, 1, n))


def _x_specs(geo, x, tm, n_grid):
    def spec(f):
        return pl.BlockSpec((tm, x[0].shape[1] if isinstance(x, tuple) else x.shape[1]),
                            (lambda i: (f(i), 0)) if n_grid == 1 else (lambda i, j: (f(i), 0)))

    if not isinstance(x, tuple):
        return (x,), [spec(lambda i: i)]
    npt = geo.P // tm
    return x, [spec(lambda i: jnp.minimum(i, npt - 1)), spec(lambda i: jnp.maximum(i - npt, 0))]


def _with_x_tile(i, npt, x_refs, fn):
    if len(x_refs) == 1:
        fn(x_refs[0])
    else:
        pl.when(i < npt)(lambda: fn(x_refs[0]))
        pl.when(i >= npt)(lambda: fn(x_refs[1]))


def _conv_in_kernel(n_x, npt, *refs):
    x_refs = refs[:n_x]
    mod_ref, ng_ref, wa_ref, wg_ref, ba_ref, bg_ref, u_ref, h_scr = refs[n_x:]

    def prologue(x_ref):
        h = _normmod(x_ref[...], ng_ref[NG_PRE_MIX:NG_PRE_MIX + 1, :],
                     mod_ref[SC1:SC1 + 1, :], mod_ref[SH1:SH1 + 1, :])
        h_scr[...] = h.astype(BF16)

    @pl.when(pl.program_id(1) == 0)
    def _():
        _with_x_tile(pl.program_id(0), npt, x_refs, prologue)

    h = h_scr[...]
    a = _dot(h, wa_ref[...]) + ba_ref[...]
    g = _dot(h, wg_ref[...]) + bg_ref[...]
    u_ref[...] = a * jax.nn.sigmoid(g)


def _conv_in_call(geo, x, mod, ng, w1, b1):
    m, d = geo.M, w1.shape[0]
    tm, tn = 512, 1024
    geo.check_tile(tm)
    nj = d // tn
    b1 = b1.reshape(1, 2 * d)
    xs, x_specs = _x_specs(geo, x, tm, 2)
    return pl.pallas_call(
        functools.partial(_conv_in_kernel, len(xs), geo.P // tm),
        grid=(m // tm, nj),
        in_specs=x_specs + [
            pl.BlockSpec((None, N_MOD, d), lambda i, j: (geo.mod_row(i, tm), 0, 0)),
            pl.BlockSpec((4, d), lambda i, j: (0, 0)),
            pl.BlockSpec((d, tn), lambda i, j: (0, j)),
            pl.BlockSpec((d, tn), lambda i, j: (0, j + nj)),
            pl.BlockSpec((1, tn), lambda i, j: (0, j)),
            pl.BlockSpec((1, tn), lambda i, j: (0, j + nj)),
        ],
        out_specs=pl.BlockSpec((tm, tn), lambda i, j: (i, j)),
        out_shape=jax.ShapeDtypeStruct((m, d), F32),
        scratch_shapes=[pltpu.VMEM((tm, d), BF16)],
        compiler_params=_cparams(("arbitrary", "arbitrary")),
        name="conv_in",
    )(*xs, mod, ng, w1, w1, b1, b1)


CONV_TM = 256
CONV_RC = 32
CONV_CC = 256


def _conv_out_kernel(n_x, tiles_p, tps_p, tps_s, *refs):
    x_refs = refs[:n_x]
    (u_ref, up_ref, un_ref, mod_ref, ng_ref, wdw_ref, bdw_ref,
     lng_ref, lnb_ref, w2_ref, b2_ref, o_ref, ext, cv, sh) = refs[n_x:]
    i = pl.program_id(0)
    is_p = i < tiles_p
    j = jnp.where(is_p, i % tps_p, (i - tiles_p) % tps_s)
    n = jnp.where(is_p, tps_p, tps_s)
    tm, d = u_ref.shape
    ext[HALO:HALO + tm, :] = u_ref[...]
    ext[0:HALO, :] = jnp.where(j == 0, 0.0, up_ref[...])
    ext[HALO + tm:HALO + tm + HALO, :] = jnp.where(j == n - 1, 0.0, un_ref[...])

    def chunk(c, carry):
        c0 = pl.multiple_of(c * CONV_CC, CONV_CC)
        e = ext[:, pl.ds(c0, CONV_CC)]
        for p in range(1, SUBLANES):
            sh[p - 1] = pltpu.roll(e, e.shape[0] - p, 0)
        for r in range(0, tm, CONV_RC):
            acc = jnp.zeros((CONV_RC, CONV_CC), F32)
            for k in range(CONV_W):
                r0 = HALO - CONV_PAD + r + k
                p = r0 % SUBLANES
                a0 = r0 - p
                if p == 0:
                    tap = ext[a0:a0 + CONV_RC, pl.ds(c0, CONV_CC)]
                else:
                    tap = sh[p - 1, a0:a0 + CONV_RC, :]
                acc = acc + tap * wdw_ref[k:k + 1, pl.ds(c0, CONV_CC)]
            cv[r:r + CONV_RC, pl.ds(c0, CONV_CC)] = acc
        return carry

    lax.fori_loop(0, d // CONV_CC, chunk, 0)

    v = cv[...] + bdw_ref[...]
    mu = jnp.mean(v, axis=-1, keepdims=True)
    xc = v - mu
    var = jnp.mean(xc * xc, axis=-1, keepdims=True)
    y = _silu(xc * lax.rsqrt(var + EPS) * lng_ref[...] + lnb_ref[...])
    out = _dot(y.astype(BF16), w2_ref[...]) + b2_ref[...]
    upd = mod_ref[G1:G1 + 1, :] * (_rms(out) * ng_ref[NG_POST_MIX:NG_POST_MIX + 1, :])

    def residual(x_ref):
        o_ref[...] = x_ref[...] + upd

    _with_x_tile(i, tiles_p, x_refs, residual)


def _conv_out_call(geo, u, x, mod, ng, wdw, bdw, lng, lnb, w2, b2):
    m, d = u.shape
    tm = CONV_TM
    assert geo.S % tm == 0 and geo.T % tm == 0
    hb = tm // HALO
    nhb = m // HALO
    xs, x_specs = _x_specs(geo, x, tm, 1)
    kern = functools.partial(_conv_out_kernel, len(xs), geo.P // tm, geo.S // tm, geo.T // tm)
    vec = lambda a: a.reshape(1, d)
    return pl.pallas_call(
        kern,
        grid=(m // tm,),
        in_specs=x_specs + [
            pl.BlockSpec((tm, d), lambda i: (i, 0)),
            pl.BlockSpec((HALO, d), lambda i: (jnp.maximum(i * hb - 1, 0), 0)),
            pl.BlockSpec((HALO, d), lambda i: (jnp.minimum((i + 1) * hb, nhb - 1), 0)),
            pl.BlockSpec((None, N_MOD, d), lambda i: (geo.mod_row(i, tm), 0, 0)),
            pl.BlockSpec((4, d), lambda i: (0, 0)),
            pl.BlockSpec((CONV_W, d), lambda i: (0, 0)),
            pl.BlockSpec((1, d), lambda i: (0, 0)),
            pl.BlockSpec((1, d), lambda i: (0, 0)),
            pl.BlockSpec((1, d), lambda i: (0, 0)),
            pl.BlockSpec((d, d), lambda i: (0, 0)),
            pl.BlockSpec((1, d), lambda i: (0, 0)),
        ],
        out_specs=pl.BlockSpec((tm, d), lambda i: (i, 0)),
        out_shape=jax.ShapeDtypeStruct((m, d), F32),
        scratch_shapes=[pltpu.VMEM((tm + 2 * HALO, d), F32), pltpu.VMEM((tm, d), F32),
                        pltpu.VMEM((SUBLANES - 1, tm + 2 * HALO, CONV_CC), F32)],
        compiler_params=_cparams(("arbitrary",)),
        name="conv_out",
    )(*xs, u, u, u, mod, ng, wdw, vec(bdw), vec(lng), vec(lnb), w2, vec(b2))


def _ffn_kernel(x_ref, mod_ref, ng_ref, wg_ref, wu_ref, wd_ref, o_ref, h_scr, acc):
    f = pl.program_id(1)

    @pl.when(f == 0)
    def _():
        h = _normmod(x_ref[...], ng_ref[NG_PRE_FFN:NG_PRE_FFN + 1, :],
                     mod_ref[SC2:SC2 + 1, :], mod_ref[SH2:SH2 + 1, :])
        h_scr[...] = h.astype(BF16)
        acc[...] = jnp.zeros_like(acc)

    h = h_scr[...]
    t = _silu(_dot(h, wg_ref[...])) * _dot(h, wu_ref[...])
    acc[...] += _dot(t.astype(BF16), wd_ref[...])

    @pl.when(f == pl.num_programs(1) - 1)
    def _():
        o_ref[...] = x_ref[...] + mod_ref[G2:G2 + 1, :] * (
            _rms(acc[...]) * ng_ref[NG_POST_FFN:NG_POST_FFN + 1, :])


def _ffn_call(geo, x, mod, ng, wg, wu, wd):
    m, d = x.shape
    dff = wg.shape[1]
    tm, tf = 512, 512
    geo.check_tile(tm)
    return pl.pallas_call(
        _ffn_kernel,
        grid=(m // tm, dff // tf),
        in_specs=[
            pl.BlockSpec((tm, d), lambda i, f: (i, 0)),
            pl.BlockSpec((None, N_MOD, d), lambda i, f: (geo.mod_row(i, tm), 0, 0)),
            pl.BlockSpec((4, d), lambda i, f: (0, 0)),
            pl.BlockSpec((d, tf), lambda i, f: (0, f)),
            pl.BlockSpec((d, tf), lambda i, f: (0, f)),
            pl.BlockSpec((tf, d), lambda i, f: (f, 0)),
        ],
        out_specs=pl.BlockSpec((tm, d), lambda i, f: (i, 0)),
        out_shape=jax.ShapeDtypeStruct((m, d), F32),
        scratch_shapes=[pltpu.VMEM((tm, d), BF16), pltpu.VMEM((tm, d), F32)],
        compiler_params=_cparams(("arbitrary", "arbitrary")),
        name="ffn",
    )(x, mod, ng, wg, wu, wd)


def _qkv_kernel(hd, scale, x_ref, mod_ref, ng_ref, w_ref, gq_ref, gk_ref, cos_ref, sin_ref,
                q_ref, k_ref, v_ref):
    h = _normmod(x_ref[...], ng_ref[NG_PRE_MIX:NG_PRE_MIX + 1, :],
                 mod_ref[SC1:SC1 + 1, :], mod_ref[SH1:SH1 + 1, :])
    r = _dot(h.astype(BF16), w_ref[...])
    nq = q_ref.shape[1]
    nk = k_ref.shape[1]
    cos = cos_ref[...]
    sin = sin_ref[...]
    for c in range(0, nq, hd):
        qh = _rope(_rms(r[:, c:c + hd]) * gq_ref[...], cos, sin, hd // 4)
        q_ref[:, c:c + hd] = (qh * scale).astype(BF16)
    for c in range(0, nk, hd):
        k_ref[:, c:c + hd] = _rope(_rms(r[:, nq + c:nq + c + hd]) * gk_ref[...], cos, sin, hd // 4)
    v_ref[...] = r[:, nq + nk:]


def _qkv_call(geo, x, mod, ng, wqkv, gq, gk, cos, sin):
    m, d = x.shape
    hd = gq.shape[-1]
    assert hd == LANES
    nq, nk = N_HEADS * hd, N_KV_HEADS * hd
    tm = 512
    geo.check_tile(tm)
    kern = functools.partial(_qkv_kernel, hd, hd ** -0.5)
    return pl.pallas_call(
        kern,
        grid=(m // tm,),
        in_specs=[
            pl.BlockSpec((tm, d), lambda i: (i, 0)),
            pl.BlockSpec((None, N_MOD, d), lambda i: (geo.mod_row(i, tm), 0, 0)),
            pl.BlockSpec((4, d), lambda i: (0, 0)),
            pl.BlockSpec((d, nq + 2 * nk), lambda i: (0, 0)),
            pl.BlockSpec((1, hd), lambda i: (0, 0)),
            pl.BlockSpec((1, hd), lambda i: (0, 0)),
            pl.BlockSpec((tm, LANES), lambda i: (geo.rope_blk(i, tm), 0)),
            pl.BlockSpec((tm, LANES), lambda i: (geo.rope_blk(i, tm), 0)),
        ],
        out_specs=[
            pl.BlockSpec((tm, nq), lambda i: (i, 0)),
            pl.BlockSpec((tm, nk), lambda i: (i, 0)),
            pl.BlockSpec((tm, nk), lambda i: (i, 0)),
        ],
        out_shape=[
            jax.ShapeDtypeStruct((m, nq), BF16),
            jax.ShapeDtypeStruct((m, nk), F32),
            jax.ShapeDtypeStruct((m, nk), F32),
        ],
        compiler_params=_cparams(("arbitrary",)),
        name="qkv",
    )(x, mod, ng, wqkv, gq.reshape(1, hd), gk.reshape(1, hd), cos, sin)


def _attn_kernel(n_group, dk, dv, shared_kv, n_kv_in, *refs):
    q_ref, kv_refs, o_ref, scr = refs[0], refs[1:1 + n_kv_in], refs[1 + n_kv_in], list(refs[2 + n_kv_in:])
    kv = []
    for r in kv_refs:
        if r.dtype == BF16:
            kv.append(r)
        else:
            s = scr.pop(0)
            pl.when(pl.program_id(2) == 0)(functools.partial(_store_bf16, s, r))
            kv.append(s)
    has_cache = n_kv_in == 4

    nt = (((1,), (1,)), ((), ()))
    for g in range(n_group):
        kc = slice(0, dk) if shared_kv else slice(g * dk, (g + 1) * dk)
        vc = slice(0, dv) if shared_kv else slice(g * dv, (g + 1) * dv)
        q = q_ref[:, g * dk:(g + 1) * dk]
        s_n = lax.dot_general(q, kv[0][:, kc], nt, preferred_element_type=F32)
        mx = jnp.max(s_n, axis=-1, keepdims=True)
        if has_cache:
            s_c = lax.dot_general(q, kv[2][:, kc], nt, preferred_element_type=F32)
            mx = jnp.maximum(mx, jnp.max(s_c, axis=-1, keepdims=True))
        p_n = jnp.exp(s_n - mx)
        den = jnp.sum(p_n, axis=-1, keepdims=True)
        o = _dot(p_n.astype(BF16), kv[1][:, vc])
        if has_cache:
            p_c = jnp.exp(s_c - mx)
            den = den + jnp.sum(p_c, axis=-1, keepdims=True)
            o = o + _dot(p_c.astype(BF16), kv[3][:, vc])
        o_ref[:, g * dv:(g + 1) * dv] = (o / den).astype(BF16)


def _store_bf16(dst, src):
    dst[...] = src[...].astype(BF16)


def _attn_call(q, kn, vn, kc, vc, *, n_batch, seq, row0, n_kv, n_group, dk, dv, tq, shared_kv, name):
    has_cache = kc is not None
    assert seq % tq == 0 and row0 % seq == 0
    nqt = seq // tq
    qb0, kb0 = row0 // tq, row0 // seq
    kw, vw = (dk, dv) if shared_kv else (n_group * dk, n_group * dv)
    in_specs = [
        pl.BlockSpec((tq, n_group * dk), lambda b, h, t: (qb0 + b * nqt + t, h)),
        pl.BlockSpec((seq, kw), lambda b, h, t: (kb0 + b, h)),
        pl.BlockSpec((seq, vw), lambda b, h, t: (kb0 + b, h)),
    ]
    args = [q, kn, vn]
    rows = [seq, seq]
    if has_cache:
        past = kc.shape[1]
        in_specs += [
            pl.BlockSpec((None, past, kw), lambda b, h, t: (b, 0, h)),
            pl.BlockSpec((None, past, vw), lambda b, h, t: (b, 0, h)),
        ]
        args += [kc, vc]
        rows += [past, past]
    scratch = [pltpu.VMEM((r, w), BF16)
               for a, r, w in zip(args[1:], rows, (kw, vw, kw, vw)) if a.dtype != BF16]
    kern = functools.partial(_attn_kernel, n_group, dk, dv, shared_kv, len(args) - 1)
    return pl.pallas_call(
        kern,
        grid=(n_batch, n_kv, nqt),
        in_specs=in_specs,
        out_specs=pl.BlockSpec((tq, n_group * dv), lambda b, h, t: (b * nqt + t, h)),
        out_shape=jax.ShapeDtypeStruct((n_batch * seq, n_kv * n_group * dv), BF16),
        scratch_shapes=scratch,
        compiler_params=_cparams(("arbitrary", "arbitrary", "arbitrary")),
        name=name,
    )(*args)


def _attn_out_kernel(npt, x_ref, op_ref, os_ref, mod_ref, ng_ref, wo_ref, y_ref):
    def project(o_ref):
        out = _dot(o_ref[...], wo_ref[...])
        y_ref[...] = x_ref[...] + mod_ref[G1:G1 + 1, :] * (
            _rms(out) * ng_ref[NG_POST_MIX:NG_POST_MIX + 1, :])

    i = pl.program_id(0)
    pl.when(i < npt)(lambda: project(op_ref))
    pl.when(i >= npt)(lambda: project(os_ref))


def _attn_out_call(geo, x, o_p, o_s, mod, ng, wo):
    m, d = x.shape
    tm = 512
    geo.check_tile(tm)
    npt = geo.P // tm
    n = o_p.shape[1]
    return pl.pallas_call(
        functools.partial(_attn_out_kernel, npt),
        grid=(m // tm,),
        in_specs=[
            pl.BlockSpec((tm, d), lambda i: (i, 0)),
            pl.BlockSpec((tm, n), lambda i: (jnp.minimum(i, npt - 1), 0)),
            pl.BlockSpec((tm, n), lambda i: (jnp.maximum(i - npt, 0), 0)),
            pl.BlockSpec((None, N_MOD, d), lambda i: (geo.mod_row(i, tm), 0, 0)),
            pl.BlockSpec((4, d), lambda i: (0, 0)),
            pl.BlockSpec(wo.shape, lambda i: (0, 0)),
        ],
        out_specs=pl.BlockSpec((tm, d), lambda i: (i, 0)),
        out_shape=jax.ShapeDtypeStruct((m, d), F32),
        compiler_params=_cparams(("arbitrary",)),
        name="attn_out",
    )(x, o_p, o_s, mod, ng, wo)


HI_MASK = 0xFFFF0000


def _as_bf16_bits(v):
    return lax.bitcast_convert_type(v.astype(BF16).astype(F32), jnp.uint32)


def _pack_rows(v, out_ref):
    n, d = v.shape
    assert d == 2 * SUBLANES * LANES
    for s in range(SUBLANES):
        lo = _as_bf16_bits(v[:, s * LANES:(s + 1) * LANES])
        hi = _as_bf16_bits(v[:, (s + SUBLANES) * LANES:(s + SUBLANES + 1) * LANES])
        out_ref[pl.ds(s, n, stride=SUBLANES), :] = (lo >> 16) | (hi & jnp.uint32(HI_MASK))


def _unpack_rows(buf, n, s):
    w = buf[pl.ds(s, n, stride=SUBLANES), :]
    lo = lax.bitcast_convert_type(w << 16, F32)
    hi = lax.bitcast_convert_type(w & jnp.uint32(HI_MASK), F32)
    return lo, hi


def _tile_copy(src_hbm, src_row, dst, dst_row, sem):
    return pltpu.make_async_copy(
        src_hbm.at[pl.ds(pl.multiple_of(src_row * SUBLANES, SUBLANES), SUBLANES), :],
        dst.at[pl.ds(pl.multiple_of(dst_row * SUBLANES, SUBLANES), SUBLANES), :], sem)


def _start_gather(idx_ref, src_hbm, dst, n, sem):
    def start(r, c):
        _tile_copy(src_hbm, idx_ref[0, r], dst, r, sem).start()
        return c

    lax.fori_loop(0, n, start, 0, unroll=8)


def _wait_gather(src_hbm, dst, n, sem):
    pltpu.make_async_copy(src_hbm.at[pl.ds(0, n * SUBLANES), :], dst, sem).wait()


def _pipelined_gather(i, n_steps, idx_ref, nxt_ref, src_hbm, bufs, n, sems, consume):
    for sl in (0, 1):
        @pl.when(i % 2 == sl)
        def _(sl=sl):
            if sl == 0:
                @pl.when(i == 0)
                def _():
                    _start_gather(idx_ref, src_hbm, bufs[0], n, sems.at[0])

            @pl.when(i + 1 < n_steps)
            def _():
                _start_gather(nxt_ref, src_hbm, bufs[1 - sl], n, sems.at[1 - sl])

            _wait_gather(src_hbm, bufs[sl], n, sems.at[sl])
            consume(bufs[sl])


def _route_kernel(n_exp, x_ref, mod_ref, ng_ref, wr_ref, br_ref, h_ref, topw_ref, topi_ref):
    h = _normmod(x_ref[...], ng_ref[NG_PRE_FFN:NG_PRE_FFN + 1, :],
                 mod_ref[SC2:SC2 + 1, :], mod_ref[SH2:SH2 + 1, :])
    _pack_rows(h, h_ref)
    logits = jnp.dot(h, wr_ref[...], preferred_element_type=F32,
                     precision=lax.Precision.HIGHEST) + br_ref[...]
    lane = lax.broadcasted_iota(jnp.int32, logits.shape, 1)
    neg = jnp.float32(-jnp.inf)
    logits = jnp.where(lane < n_exp, logits, neg)
    m1 = jnp.max(logits, axis=-1, keepdims=True)
    i1 = jnp.min(jnp.where(logits == m1, lane, LANES), axis=-1, keepdims=True)
    rest = jnp.where(lane == i1, neg, logits)
    m2 = jnp.max(rest, axis=-1, keepdims=True)
    i2 = jnp.min(jnp.where(rest == m2, lane, LANES), axis=-1, keepdims=True)
    e = jnp.exp(m2 - m1)
    w1 = 1.0 / (1.0 + e)
    w2 = e / (1.0 + e)
    topw_ref[...] = jnp.where(lane == 0, w1, jnp.where(lane == 1, w2, 0.0))
    topi_ref[...] = jnp.where(lane == 0, i1, jnp.where(lane == 1, i2, 0))


def _route_call(geo, x, mod, ng, wr_pad, br_pad):
    m, d = x.shape
    tm = 512
    geo.check_tile(tm)
    kern = functools.partial(_route_kernel, N_EXPERTS)
    return pl.pallas_call(
        kern,
        grid=(m // tm,),
        in_specs=[
            pl.BlockSpec((tm, d), lambda i: (i, 0)),
            pl.BlockSpec((None, N_MOD, d), lambda i: (geo.mod_row(i, tm), 0, 0)),
            pl.BlockSpec((4, d), lambda i: (0, 0)),
            pl.BlockSpec((d, LANES), lambda i: (0, 0)),
            pl.BlockSpec((1, LANES), lambda i: (0, 0)),
        ],
        out_specs=[
            pl.BlockSpec((tm * SUBLANES, LANES), lambda i: (i, 0)),
            pl.BlockSpec((tm, LANES), lambda i: (i, 0)),
            pl.BlockSpec((tm, LANES), lambda i: (i, 0)),
        ],
        out_shape=[
            jax.ShapeDtypeStruct((m * SUBLANES, LANES), jnp.uint32),
            jax.ShapeDtypeStruct((m, LANES), F32),
            jax.ShapeDtypeStruct((m, LANES), jnp.int32),
        ],
        compiler_params=_cparams(("arbitrary",)),
        name="route",
    )(x, mod, ng, wr_pad, br_pad)


def _dispatch(topi, topw, tr):
    m = topi.shape[0]
    n_asg = TOP_K * m
    n_tiles = n_asg // tr + N_EXPERTS
    e = topi[:, :TOP_K].reshape(n_asg)
    w = topw[:, :TOP_K].reshape(n_asg)
    onehot = (e[:, None] == jnp.arange(N_EXPERTS, dtype=jnp.int32)[None, :]).astype(jnp.int32)
    csum = jnp.cumsum(onehot, axis=0)
    count = csum[-1]
    rank = jnp.sum(csum * onehot, axis=1) - 1
    ptiles = (count + tr - 1) // tr
    tile_end = jnp.cumsum(ptiles)
    tile_start = tile_end - ptiles
    n_used = tile_end[-1:]
    dest = tile_start[e] * tr + rank
    asg = jnp.full((n_tiles * tr,), -1, jnp.int32).at[dest].set(jnp.arange(n_asg, dtype=jnp.int32))
    src = jnp.maximum(asg, 0) // TOP_K
    gate = jnp.where(asg >= 0, w[jnp.maximum(asg, 0)], 0.0)
    tile_expert = jnp.minimum(
        jnp.sum((jnp.arange(n_tiles, dtype=jnp.int32)[:, None] >= tile_end[None, :]).astype(jnp.int32), axis=1),
        N_EXPERTS - 1)
    return src, gate, tile_expert, n_used.astype(jnp.int32), dest


def _moe_kernel(te_ref, nu_ref, src_ref, nxt_ref, h_hbm, g_ref, wg_ref, wu_ref, wd_ref, y_ref,
                buf0, buf1, hbf, acc, sems):
    i = pl.program_id(0)
    f = pl.program_id(1)
    n_used = nu_ref[0]
    used = i < n_used
    tr = hbf.shape[0]

    def unpack(buf):
        for s in range(SUBLANES):
            lo, hi = _unpack_rows(buf, tr, s)
            hbf[:, s * LANES:(s + 1) * LANES] = lo.astype(BF16)
            hbf[:, (s + SUBLANES) * LANES:(s + SUBLANES + 1) * LANES] = hi.astype(BF16)

    @pl.when(used & (f == 0))
    def _():
        _pipelined_gather(i, n_used, src_ref, nxt_ref, h_hbm, (buf0, buf1), tr, sems, unpack)
        acc[...] = jnp.zeros_like(acc)

    @pl.when(used)
    def _():
        h = hbf[...]
        t = _silu(_dot(h, wg_ref[...])) * _dot(h, wu_ref[...]) * g_ref[...]
        acc[...] += _dot(t.astype(BF16), wd_ref[...])

    last = f == pl.num_programs(1) - 1

    @pl.when(used & last)
    def _():
        _pack_rows(acc[...], y_ref)

    @pl.when(jnp.logical_not(used) & last)
    def _():
        y_ref[...] = jnp.zeros_like(y_ref)


def _moe_call(h, src, gate, tile_expert, n_used, wg, wu, wd, tr):
    n_exp, d, dfe = wg.shape
    n_tiles = tile_expert.shape[0]
    tf = 256
    assert dfe % tf == 0
    nf = dfe // tf

    def fblk(i, f, nu):
        return jnp.where(i < nu[0], f, nf - 1)

    src = src.reshape(n_tiles, 1, tr)
    tile_rows = tr * SUBLANES
    grid_spec = pltpu.PrefetchScalarGridSpec(
        num_scalar_prefetch=2,
        grid=(n_tiles, nf),
        in_specs=[
            pl.BlockSpec((None, 1, tr), lambda i, f, te, nu: (i, 0, 0), memory_space=pltpu.SMEM),
            pl.BlockSpec((None, 1, tr), lambda i, f, te, nu: (jnp.minimum(i + 1, n_tiles - 1), 0, 0),
                         memory_space=pltpu.SMEM),
            pl.BlockSpec(memory_space=pl.ANY),
            pl.BlockSpec((tr, 1), lambda i, f, te, nu: (i, 0)),
            pl.BlockSpec((None, d, tf), lambda i, f, te, nu: (te[i], 0, fblk(i, f, nu))),
            pl.BlockSpec((None, d, tf), lambda i, f, te, nu: (te[i], 0, fblk(i, f, nu))),
            pl.BlockSpec((None, tf, d), lambda i, f, te, nu: (te[i], fblk(i, f, nu), 0)),
        ],
        out_specs=pl.BlockSpec((tile_rows, LANES), lambda i, f, te, nu: (i, 0)),
        scratch_shapes=[pltpu.VMEM((tile_rows, LANES), jnp.uint32), pltpu.VMEM((tile_rows, LANES), jnp.uint32),
                        pltpu.VMEM((tr, d), BF16), pltpu.VMEM((tr, d), F32),
                        pltpu.SemaphoreType.DMA((2,))],
    )
    return pl.pallas_call(
        _moe_kernel,
        grid_spec=grid_spec,
        out_shape=jax.ShapeDtypeStruct((n_tiles * tile_rows, LANES), jnp.uint32),
        compiler_params=_cparams(("arbitrary", "arbitrary")),
        name="moe",
    )(tile_expert, n_used, src, src, h, gate.reshape(n_tiles * tr, 1), wg, wu, wd)


def _combine_kernel(npt, pos_ref, nxt_ref, y_hbm, x_ref, mod_ref, ng_ref, *refs):
    outs, (buf0, buf1, ysum, sems) = refs[:-4], refs[-4:]
    i = pl.program_id(0)
    tm = x_ref.shape[0]

    def add_pairs(buf):
        for s in range(SUBLANES):
            lo, hi = _unpack_rows(buf, TOP_K * tm, s)
            ysum[:, s * LANES:(s + 1) * LANES] = lo[:tm] + lo[tm:]
            ysum[:, (s + SUBLANES) * LANES:(s + SUBLANES + 1) * LANES] = hi[:tm] + hi[tm:]

    _pipelined_gather(i, pl.num_programs(0), pos_ref, nxt_ref, y_hbm, (buf0, buf1), TOP_K * tm, sems,
                      add_pairs)
    res = x_ref[...] + mod_ref[G2:G2 + 1, :] * (
        _rms(ysum[...]) * ng_ref[NG_POST_FFN:NG_POST_FFN + 1, :])
    if npt is None:
        outs[0][...] = res
    else:
        @pl.when(i < npt)
        def _():
            outs[0][...] = res

        @pl.when(i >= npt)
        def _():
            outs[1][...] = res


def _combine_call(geo, x, y, dest, mod, ng, split):
    m, d = x.shape
    tm = 256
    geo.check_tile(tm)
    nt = m // tm
    npt = geo.P // tm
    pos = dest.reshape(nt, tm, TOP_K).transpose(0, 2, 1).reshape(nt, 1, TOP_K * tm)
    if split:
        out_specs = [pl.BlockSpec((tm, d), lambda i: (jnp.minimum(i, npt - 1), 0)),
                     pl.BlockSpec((tm, d), lambda i: (jnp.maximum(i - npt, 0), 0))]
        out_shape = [jax.ShapeDtypeStruct((geo.P, d), F32), jax.ShapeDtypeStruct((m - geo.P, d), F32)]
    else:
        out_specs = pl.BlockSpec((tm, d), lambda i: (i, 0))
        out_shape = jax.ShapeDtypeStruct((m, d), F32)
    buf = pltpu.VMEM((TOP_K * tm * SUBLANES, LANES), jnp.uint32)
    return pl.pallas_call(
        functools.partial(_combine_kernel, npt if split else None),
        grid=(nt,),
        in_specs=[
            pl.BlockSpec((None, 1, TOP_K * tm), lambda i: (i, 0, 0), memory_space=pltpu.SMEM),
            pl.BlockSpec((None, 1, TOP_K * tm), lambda i: (jnp.minimum(i + 1, nt - 1), 0, 0),
                         memory_space=pltpu.SMEM),
            pl.BlockSpec(memory_space=pl.ANY),
            pl.BlockSpec((tm, d), lambda i: (i, 0)),
            pl.BlockSpec((None, N_MOD, d), lambda i: (geo.mod_row(i, tm), 0, 0)),
            pl.BlockSpec((4, d), lambda i: (0, 0)),
        ],
        out_specs=out_specs,
        out_shape=out_shape,
        scratch_shapes=[buf, buf, pltpu.VMEM((tm, d), F32), pltpu.SemaphoreType.DMA((2,))],
        compiler_params=_cparams(("arbitrary",)),
        name="moe_combine",
    )(pos, pos, y, x, mod, ng)


def _mla_in_kernel(scale, x_ref, mod_ref, ng_ref, wd_ref, gq_ref, gkv_ref, wuq_ref, wuk_ref, wuv_ref,
                   cos_ref, sin_ref, q_ref, kcat_ref, v_ref, ckv_ref, kpe_ref):
    h = _normmod(x_ref[...], ng_ref[NG_PRE_MIX:NG_PRE_MIX + 1, :],
                 mod_ref[SC1:SC1 + 1, :], mod_ref[SH1:SH1 + 1, :])
    r = _dot(h.astype(BF16), wd_ref[...])
    qr = gq_ref.shape[1]
    kr = gkv_ref.shape[1]
    rope_dim = kpe_ref.shape[1]
    cos = cos_ref[...]
    sin = sin_ref[...]
    cq = (_rms(r[:, :qr]) * gq_ref[...]).astype(BF16)
    ckv = _rms(r[:, qr:qr + kr]) * gkv_ref[...]
    kpe = r[:, qr + kr:qr + kr + LANES]
    ckv_ref[...] = ckv
    kpe_ref[...] = kpe[:, :rope_dim]
    ckv_b = ckv.astype(BF16)
    q = _dot(cq, wuq_ref[...])
    kn = _dot(ckv_b, wuk_ref[...])
    v_ref[...] = _dot(ckv_b, wuv_ref[...]).astype(BF16)
    kpe_r = _rope(kpe, cos, sin, rope_dim // 4).astype(BF16)
    for hh in range(q_ref.shape[1] // MLA_QK):
        c = hh * MLA_QK
        q_ref[:, c:c + NOPE_DIM] = (q[:, c:c + NOPE_DIM] * scale).astype(BF16)
        q_ref[:, c + NOPE_DIM:c + MLA_QK] = (
            _rope(q[:, c + NOPE_DIM:c + MLA_QK], cos, sin, rope_dim // 4) * scale).astype(BF16)
        kcat_ref[:, c:c + NOPE_DIM] = kn[:, hh * NOPE_DIM:(hh + 1) * NOPE_DIM].astype(BF16)
        kcat_ref[:, c + NOPE_DIM:c + MLA_QK] = kpe_r


def _mla_in_call(geo, x, mod, ng, wd_cat, gq, gkv, wuq_pad, wuk, wuv, cos, sin):
    m, d = x.shape
    qr, kr = gq.shape[-1], gkv.shape[-1]
    nqk = wuq_pad.shape[1]
    nv = wuv.shape[1]
    tm = 512
    geo.check_tile(tm)
    kern = functools.partial(_mla_in_kernel, (NOPE_DIM + ROPE_DIM) ** -0.5)
    full = lambda a: pl.BlockSpec(a.shape, lambda i: (0, 0))
    row = lambda n: pl.BlockSpec((tm, n), lambda i: (i, 0))
    return pl.pallas_call(
        kern,
        grid=(m // tm,),
        in_specs=[
            row(d),
            pl.BlockSpec((None, N_MOD, d), lambda i: (geo.mod_row(i, tm), 0, 0)),
            pl.BlockSpec((4, d), lambda i: (0, 0)),
            full(wd_cat),
            pl.BlockSpec((1, qr), lambda i: (0, 0)),
            pl.BlockSpec((1, kr), lambda i: (0, 0)),
            full(wuq_pad), full(wuk), full(wuv),
            pl.BlockSpec((tm, LANES), lambda i: (geo.rope_blk(i, tm), 0)),
            pl.BlockSpec((tm, LANES), lambda i: (geo.rope_blk(i, tm), 0)),
        ],
        out_specs=[row(nqk), row(nqk), row(nv), row(kr), row(ROPE_DIM)],
        out_shape=[
            jax.ShapeDtypeStruct((m, nqk), BF16),
            jax.ShapeDtypeStruct((m, nqk), BF16),
            jax.ShapeDtypeStruct((m, nv), BF16),
            jax.ShapeDtypeStruct((m, kr), F32),
            jax.ShapeDtypeStruct((m, ROPE_DIM), F32),
        ],
        compiler_params=_cparams(("arbitrary",)),
        name="mla_in",
    )(x, mod, ng, wd_cat, gq.reshape(1, qr), gkv.reshape(1, kr), wuq_pad, wuk, wuv, cos, sin)


def _mla_cache_kernel(ckv_ref, kpe_ref, wuk_ref, wuv_ref, kcat_ref, v_ref):
    ckv = ckv_ref[...].astype(BF16)
    kn = _dot(ckv, wuk_ref[...])
    v_ref[...] = _dot(ckv, wuv_ref[...]).astype(BF16)
    kpe = kpe_ref[...].astype(BF16)
    for hh in range(kcat_ref.shape[1] // MLA_QK):
        c = hh * MLA_QK
        kcat_ref[:, c:c + NOPE_DIM] = kn[:, hh * NOPE_DIM:(hh + 1) * NOPE_DIM].astype(BF16)
        kcat_ref[:, c + NOPE_DIM:c + MLA_QK] = kpe


def _mla_cache_call(ckv, kpe_pad, wuk, wuv):
    n, kr = ckv.shape
    tm = 512
    assert n % tm == 0
    nqk = MLA_HEADS * MLA_QK
    nv = wuv.shape[1]
    return pl.pallas_call(
        _mla_cache_kernel,
        grid=(n // tm,),
        in_specs=[
            pl.BlockSpec((tm, kr), lambda i: (i, 0)),
            pl.BlockSpec((tm, LANES), lambda i: (i, 0)),
            pl.BlockSpec(wuk.shape, lambda i: (0, 0)),
            pl.BlockSpec(wuv.shape, lambda i: (0, 0)),
        ],
        out_specs=[pl.BlockSpec((tm, nqk), lambda i: (i, 0)), pl.BlockSpec((tm, nv), lambda i: (i, 0))],
        out_shape=[jax.ShapeDtypeStruct((n, nqk), BF16), jax.ShapeDtypeStruct((n, nv), BF16)],
        compiler_params=_cparams(("arbitrary",)),
        name="mla_cache",
    )(ckv, kpe_pad, wuk, wuv)


def _rope_table(n_tokens, dim, n_ident):
    rows = n_tokens // GRID_W
    row = jnp.repeat(jnp.arange(rows, dtype=F32), GRID_W)
    col = jnp.tile(jnp.arange(GRID_W, dtype=F32), rows)
    half = dim // 2
    inv = ROPE_THETA ** (-jnp.arange(0, half, 2, dtype=F32) / half)
    ar = row[:, None] * inv[None, :]
    ac = col[:, None] * inv[None, :]
    ang = jnp.concatenate([ar, ar, ac, ac], axis=-1)
    cos = jnp.cos(ang)
    sin = jnp.sin(ang)
    hb = dim // 4
    first = (jnp.arange(dim) % (2 * hb)) < hb
    sin = jnp.where(first[None, :], -sin, sin)
    cos = jnp.pad(cos, ((0, n_ident), (0, LANES - dim)), constant_values=1.0)
    sin = jnp.pad(sin, ((0, n_ident), (0, LANES - dim)))
    return cos, sin


def kernel(x_prompt, x_sample, cache_gqa_k, cache_gqa_v, cache_mla_ckv, cache_mla_kpe, c, c_ctx,
           w_mod, b_mod, norm_g,
           conv_w1, conv_b1, conv_wdw, conv_bdw, conv_ln_g, conv_ln_b, conv_w2, conv_b2,
           gqa_wq, gqa_wk, gqa_wv, gqa_gq, gqa_gk, gqa_wo,
           mla_wdq, mla_gq, mla_wuq, mla_wdkv, mla_gkv, mla_wuk, mla_wuv, mla_wo,
           ffn_wg, ffn_wu, ffn_wd,
           moe_wr, moe_br, moe_wg, moe_wu, moe_wd):
    b, s, d = x_prompt.shape
    db, t, _ = x_sample.shape
    geo = _Geom(b, s, db, t)
    depth = w_mod.shape[0]
    past = cache_gqa_k.shape[2]
    hd = gqa_gq.shape[-1]
    bf = lambda a: a.astype(BF16)

    assert N_MIXERS == 3
    x = (x_prompt.reshape(geo.P, d), x_sample.reshape(db * t, d))

    n_cond = 1 + db
    cond = jnp.concatenate([c_ctx[None, :], c], axis=0)
    cond = jnp.pad(cond, ((0, -n_cond % 8), (0, 0)))
    mods = _mod_call(cond, w_mod, b_mod).reshape(depth, cond.shape[0], N_MOD, d)

    rope_tm = 512
    cos_hd, sin_hd = _rope_table(t, hd, rope_tm)
    cos_pe, sin_pe = _rope_table(t, ROPE_DIM, rope_tm)

    ks, vs, cs, ps = [], [], [], []
    for i in range(depth):
        mod, ng = mods[i], norm_g[i]
        j = i // N_MIXERS
        if i % N_MIXERS == 0:
            u = _conv_in_call(geo, x, mod, ng, bf(conv_w1[j]), conv_b1[j])
            x = _conv_out_call(geo, u, x, mod, ng, conv_wdw[j], conv_bdw[j], conv_ln_g[j],
                               conv_ln_b[j], bf(conv_w2[j]), conv_b2[j])
        elif i % N_MIXERS == 1:
            wqkv = bf(jnp.concatenate([gqa_wq[j], gqa_wk[j], gqa_wv[j]], axis=1))
            q, k, v = _qkv_call(geo, x, mod, ng, wqkv, gqa_gq[j], gqa_gk[j], cos_hd, sin_hd)
            ks.append(k[:geo.P].reshape(b, s, N_KV_HEADS, hd))
            vs.append(v[:geo.P].reshape(b, s, N_KV_HEADS, hd))
            kw = dict(n_kv=N_KV_HEADS, n_group=N_HEADS // N_KV_HEADS, dk=hd, dv=hd, shared_kv=True)
            o_p = _attn_call(q, k, v, None, None, n_batch=b, seq=s, row0=0, tq=min(s, 256),
                             name="gqa_attn_prompt", **kw)
            o_s = _attn_call(q, k, v,
                             cache_gqa_k[:, j].reshape(db, past, N_KV_HEADS * hd),
                             cache_gqa_v[:, j].reshape(db, past, N_KV_HEADS * hd),
                             n_batch=db, seq=t, row0=geo.P, tq=256, name="gqa_attn_sample", **kw)
            x = _attn_out_call(geo, x, o_p, o_s, mod, ng, bf(gqa_wo[j]))
        else:
            qr, kr = mla_gq.shape[-1], mla_gkv.shape[-1]
            wd_cat = bf(jnp.pad(jnp.concatenate([mla_wdq[j], mla_wdkv[j]], axis=1),
                                ((0, 0), (0, LANES - ROPE_DIM))))
            wuq_pad = bf(jnp.pad(mla_wuq[j], ((0, 0), (0, 0), (0, MLA_QK - NOPE_DIM - ROPE_DIM)))
                         ).reshape(qr, MLA_HEADS * MLA_QK)
            wuk = bf(mla_wuk[j]).reshape(kr, MLA_HEADS * NOPE_DIM)
            wuv = bf(mla_wuv[j]).reshape(kr, MLA_HEADS * V_DIM)
            q, kcat, v, ckv, kpe = _mla_in_call(geo, x, mod, ng, wd_cat, mla_gq[j], mla_gkv[j],
                                                wuq_pad, wuk, wuv, cos_pe, sin_pe)
            cs.append(ckv[:geo.P].reshape(b, s, kr))
            ps.append(kpe[:geo.P].reshape(b, s, ROPE_DIM))
            kc, vc = _mla_cache_call(
                cache_mla_ckv[:, j].reshape(db * past, kr),
                jnp.pad(cache_mla_kpe[:, j].reshape(db * past, ROPE_DIM), ((0, 0), (0, LANES - ROPE_DIM))),
                wuk, wuv)
            kw = dict(n_kv=MLA_HEADS // MLA_HPS, n_group=MLA_HPS, dk=MLA_QK, dv=V_DIM, shared_kv=False)
            o_p = _attn_call(q, kcat, v, None, None, n_batch=b, seq=s, row0=0, tq=min(s, 256),
                             name="mla_attn_prompt", **kw)
            o_s = _attn_call(q, kcat, v, kc.reshape(db, past, -1), vc.reshape(db, past, -1),
                             n_batch=db, seq=t, row0=geo.P, tq=256, name="mla_attn_sample", **kw)
            x = _attn_out_call(geo, x, o_p, o_s, mod, ng, bf(mla_wo[j]))
        f = i // 2
        if i % 2 == 0:
            x = _ffn_call(geo, x, mod, ng, bf(ffn_wg[f]), bf(ffn_wu[f]), bf(ffn_wd[f]))
        else:
            wr_pad = jnp.pad(moe_wr[f], ((0, 0), (0, LANES - N_EXPERTS)))
            br_pad = jnp.pad(moe_br[f], (0, LANES - N_EXPERTS)).reshape(1, LANES)
            h, topw, topi = _route_call(geo, x, mod, ng, wr_pad, br_pad)
            src, gate, tile_expert, n_used, dest = _dispatch(topi, topw, MOE_TR)
            y = _moe_call(h, src, gate, tile_expert, n_used,
                          bf(moe_wg[f]), bf(moe_wu[f]), bf(moe_wd[f]), MOE_TR)
            x = _combine_call(geo, x, y, dest, mod, ng, split=(i == depth - 1))

    y_prompt, y_sample = x if isinstance(x, (list, tuple)) else (x[:geo.P], x[geo.P:])
    return (y_prompt.reshape(b, s, d), y_sample.reshape(db, t, d), jnp.stack(ks, axis=1), jnp.stack(vs, axis=1),
            jnp.stack(cs, axis=1), jnp.stack(ps, axis=1))
```

```python
import functools

import jax
import jax.numpy as jnp
from jax import lax
from jax.experimental import pallas as pl
from jax.experimental.pallas import tpu as pltpu

F32 = jnp.float32
BF16 = jnp.bfloat16

EPS = 1e-6
GRID_W = 64
ROPE_THETA = 10000.0
N_MIXERS = 3
N_MOD = 6
CONV_W = 31
CONV_PAD = CONV_W // 2
N_HEADS = 16
N_KV_HEADS = 4
MLA_HEADS = 16
NOPE_DIM = 128
ROPE_DIM = 64
V_DIM = 128
N_EXPERTS = 8
TOP_K = 2
MOE_TR = 512
MOE_TF = 1408
MOE_TC = 256

LANES = 128
SUBLANES = 8
HALO = 16
MLA_QK = 256
MLA_HPS = 4
VMEM_LIMIT = 56 * 1024 * 1024

SH1, SC1, G1, SH2, SC2, G2 = range(6)
NG_PRE_MIX, NG_POST_MIX, NG_PRE_FFN, NG_POST_FFN = range(4)


def _cparams(sem):
    return pltpu.CompilerParams(dimension_semantics=sem, vmem_limit_bytes=VMEM_LIMIT)


def _rms(x):
    return x * lax.rsqrt(jnp.mean(x * x, axis=-1, keepdims=True) + EPS)


def _normmod(x, g, sc, sh):
    return (_rms(x) * g) * (1.0 + sc) + sh


def _silu(x):
    return x * jax.nn.sigmoid(x)


def _dot(a, b):
    return jnp.dot(a, b, preferred_element_type=F32)


def _rope(x, cos, sin_signed, hb):
    lane = lax.broadcasted_iota(jnp.int32, x.shape, 1)
    first = (lane % (2 * hb)) < hb
    rot = jnp.where(first, pltpu.roll(x, LANES - hb, 1), pltpu.roll(x, hb, 1))
    return x * cos + rot * sin_signed


class _Geom:
    def __init__(self, b, s, db, t):
        self.B, self.S, self.DB, self.T = b, s, db, t
        self.P = b * s
        self.M = self.P + db * t

    def check_tile(self, tm):
        assert self.P % tm == 0 and self.T % tm == 0, (self.P, self.T, tm)

    def mod_row(self, i, tm):
        npt = self.P // tm
        return jnp.where(i < npt, 0, 1 + (i - npt) // (self.T // tm))

    def rope_blk(self, i, tm):
        npt = self.P // tm
        return jnp.where(i < npt, self.T // tm, (i - npt) % (self.T // tm))


def _mod_kernel(c_ref, w_ref, b_ref, o_ref):
    c = c_ref[...]
    o_ref[...] = _dot(_silu(c).astype(BF16), w_ref[...].astype(BF16)) + b_ref[...]


def _mod_call(cond, w_mod, b_mod):
    depth, d, n = w_mod.shape
    rows = cond.shape[0]
    tn = 1024
    return pl.pallas_call(
        _mod_kernel,
        grid=(depth, n // tn),
        in_specs=[
            pl.BlockSpec((rows, d), lambda l, j: (0, 0)),
            pl.BlockSpec((None, d, tn), lambda l, j: (l, 0, j)),
            pl.BlockSpec((None, 1, tn), lambda l, j: (l, 0, j)),
        ],
        out_specs=pl.BlockSpec((None, rows, tn), lambda l, j: (l, 0, j)),
        out_shape=jax.ShapeDtypeStruct((depth, rows, n), F32),
        compiler_params=_cparams(("arbitrary", "arbitrary")),
        name="mod",
    )(cond, w_mod, b_mod.reshape(depth, 1, n))


def _x_specs(geo, x, tm, n_grid):
    def spec(f):
        return pl.BlockSpec((tm, x[0].shape[1] if isinstance(x, tuple) else x.shape[1]),
                            (lambda i: (f(i), 0)) if n_grid == 1 else (lambda i, j: (f(i), 0)))

    if not isinstance(x, tuple):
        return (x,), [spec(lambda i: i)]
    npt = geo.P // tm
    return x, [spec(lambda i: jnp.minimum(i, npt - 1)), spec(lambda i: jnp.maximum(i - npt, 0))]


def _with_x_tile(i, npt, x_refs, fn):
    if len(x_refs) == 1:
        fn(x_refs[0])
    else:
        pl.when(i < npt)(lambda: fn(x_refs[0]))
        pl.when(i >= npt)(lambda: fn(x_refs[1]))


def _conv_in_kernel(n_x, npt, *refs):
    x_refs = refs[:n_x]
    mod_ref, ng_ref, wa_ref, wg_ref, ba_ref, bg_ref, u_ref, h_scr = refs[n_x:]

    def prologue(x_ref):
        h = _normmod(x_ref[...], ng_ref[NG_PRE_MIX:NG_PRE_MIX + 1, :],
                     mod_ref[SC1:SC1 + 1, :], mod_ref[SH1:SH1 + 1, :])
        h_scr[...] = h.astype(BF16)

    @pl.when(pl.program_id(1) == 0)
    def _():
        _with_x_tile(pl.program_id(0), npt, x_refs, prologue)

    h = h_scr[...]
    a = _dot(h, wa_ref[...]) + ba_ref[...]
    g = _dot(h, wg_ref[...]) + bg_ref[...]
    u_ref[...] = a * jax.nn.sigmoid(g)


def _conv_in_call(geo, x, mod, ng, w1, lyr, b1):
    m, d = geo.M, w1.shape[1]
    tm, tn = 512, 1024
    geo.check_tile(tm)
    nj = d // tn
    b1 = b1.reshape(1, 2 * d)
    xs, x_specs = _x_specs(geo, x, tm, 2)
    return pl.pallas_call(
        functools.partial(_conv_in_kernel, len(xs), geo.P // tm),
        grid=(m // tm, nj),
        in_specs=x_specs + [
            pl.BlockSpec((None, N_MOD, d), lambda i, j: (geo.mod_row(i, tm), 0, 0)),
            pl.BlockSpec((4, d), lambda i, j: (0, 0)),
            pl.BlockSpec((None, d, tn), lambda i, j: (lyr, 0, j)),
            pl.BlockSpec((None, d, tn), lambda i, j: (lyr, 0, j + nj)),
            pl.BlockSpec((1, tn), lambda i, j: (0, j)),
            pl.BlockSpec((1, tn), lambda i, j: (0, j + nj)),
        ],
        out_specs=pl.BlockSpec((tm, tn), lambda i, j: (i, j)),
        out_shape=jax.ShapeDtypeStruct((m, d), F32),
        scratch_shapes=[pltpu.VMEM((tm, d), BF16)],
        compiler_params=_cparams(("arbitrary", "arbitrary")),
        name="conv_in",
    )(*xs, mod, ng, w1, w1, b1, b1)


CONV_TM = 256
CONV_RC = 32
CONV_CC = 256


def _conv_out_kernel(n_x, tiles_p, tps_p, tps_s, *refs):
    x_refs = refs[:n_x]
    (u_ref, up_ref, un_ref, mod_ref, ng_ref, wdw_ref, bdw_ref,
     lng_ref, lnb_ref, w2_ref, b2_ref, o_ref, ext, cv, sh) = refs[n_x:]
    i = pl.program_id(0)
    is_p = i < tiles_p
    j = jnp.where(is_p, i % tps_p, (i - tiles_p) % tps_s)
    n = jnp.where(is_p, tps_p, tps_s)
    tm, d = u_ref.shape
    ext[HALO:HALO + tm, :] = u_ref[...]
    ext[0:HALO, :] = jnp.where(j == 0, 0.0, up_ref[...])
    ext[HALO + tm:HALO + tm + HALO, :] = jnp.where(j == n - 1, 0.0, un_ref[...])

    def chunk(c, carry):
        c0 = pl.multiple_of(c * CONV_CC, CONV_CC)
        e = ext[:, pl.ds(c0, CONV_CC)]
        for p in range(1, SUBLANES):
            sh[p - 1] = pltpu.roll(e, e.shape[0] - p, 0)
        for r in range(0, tm, CONV_RC):
            acc = jnp.zeros((CONV_RC, CONV_CC), F32)
            for k in range(CONV_W):
                r0 = HALO - CONV_PAD + r + k
                p = r0 % SUBLANES
                a0 = r0 - p
                if p == 0:
                    tap = ext[a0:a0 + CONV_RC, pl.ds(c0, CONV_CC)]
                else:
                    tap = sh[p - 1, a0:a0 + CONV_RC, :]
                acc = acc + tap * wdw_ref[k:k + 1, pl.ds(c0, CONV_CC)]
            cv[r:r + CONV_RC, pl.ds(c0, CONV_CC)] = acc
        return carry

    lax.fori_loop(0, d // CONV_CC, chunk, 0)

    v = cv[...] + bdw_ref[...]
    mu = jnp.mean(v, axis=-1, keepdims=True)
    xc = v - mu
    var = jnp.mean(xc * xc, axis=-1, keepdims=True)
    y = _silu(xc * lax.rsqrt(var + EPS) * lng_ref[...] + lnb_ref[...])
    out = _dot(y.astype(BF16), w2_ref[...]) + b2_ref[...]
    upd = mod_ref[G1:G1 + 1, :] * (_rms(out) * ng_ref[NG_POST_MIX:NG_POST_MIX + 1, :])

    def residual(x_ref):
        o_ref[...] = x_ref[...] + upd

    _with_x_tile(i, tiles_p, x_refs, residual)


def _conv_out_call(geo, u, x, mod, ng, wdw, bdw, lng, lnb, w2, lyr, b2):
    m, d = u.shape
    tm = CONV_TM
    assert geo.S % tm == 0 and geo.T % tm == 0
    hb = tm // HALO
    nhb = m // HALO
    xs, x_specs = _x_specs(geo, x, tm, 1)
    kern = functools.partial(_conv_out_kernel, len(xs), geo.P // tm, geo.S // tm, geo.T // tm)
    vec = lambda a: a.reshape(1, d)
    return pl.pallas_call(
        kern,
        grid=(m // tm,),
        in_specs=x_specs + [
            pl.BlockSpec((tm, d), lambda i: (i, 0)),
            pl.BlockSpec((HALO, d), lambda i: (jnp.maximum(i * hb - 1, 0), 0)),
            pl.BlockSpec((HALO, d), lambda i: (jnp.minimum((i + 1) * hb, nhb - 1), 0)),
            pl.BlockSpec((None, N_MOD, d), lambda i: (geo.mod_row(i, tm), 0, 0)),
            pl.BlockSpec((4, d), lambda i: (0, 0)),
            pl.BlockSpec((CONV_W, d), lambda i: (0, 0)),
            pl.BlockSpec((1, d), lambda i: (0, 0)),
            pl.BlockSpec((1, d), lambda i: (0, 0)),
            pl.BlockSpec((1, d), lambda i: (0, 0)),
            pl.BlockSpec((None, d, d), lambda i: (lyr, 0, 0)),
            pl.BlockSpec((1, d), lambda i: (0, 0)),
        ],
        out_specs=pl.BlockSpec((tm, d), lambda i: (i, 0)),
        out_shape=jax.ShapeDtypeStruct((m, d), F32),
        scratch_shapes=[pltpu.VMEM((tm + 2 * HALO, d), F32), pltpu.VMEM((tm, d), F32),
                        pltpu.VMEM((SUBLANES - 1, tm + 2 * HALO, CONV_CC), F32)],
        compiler_params=_cparams(("arbitrary",)),
        name="conv_out",
    )(*xs, u, u, u, mod, ng, wdw, vec(bdw), vec(lng), vec(lnb), w2, vec(b2))


def _ffn_kernel(x_ref, mod_ref, ng_ref, wg_ref, wu_ref, wd_ref, o_ref, h_scr, acc):
    f = pl.program_id(1)

    @pl.when(f == 0)
    def _():
        h = _normmod(x_ref[...], ng_ref[NG_PRE_FFN:NG_PRE_FFN + 1, :],
                     mod_ref[SC2:SC2 + 1, :], mod_ref[SH2:SH2 + 1, :])
        h_scr[...] = h.astype(BF16)
        acc[...] = jnp.zeros_like(acc)

    h = h_scr[...]
    t = _silu(_dot(h, wg_ref[...])) * _dot(h, wu_ref[...])
    acc[...] += _dot(t.astype(BF16), wd_ref[...])

    @pl.when(f == pl.num_programs(1) - 1)
    def _():
        o_ref[...] = x_ref[...] + mod_ref[G2:G2 + 1, :] * (
            _rms(acc[...]) * ng_ref[NG_POST_FFN:NG_POST_FFN + 1, :])


def _ffn_call(geo, x, mod, ng, wg, wu, wd, lyr):
    m, d = x.shape
    dff = wg.shape[2]
    tm, tf = 512, 512
    geo.check_tile(tm)
    return pl.pallas_call(
        _ffn_kernel,
        grid=(m // tm, dff // tf),
        in_specs=[
            pl.BlockSpec((tm, d), lambda i, f: (i, 0)),
            pl.BlockSpec((None, N_MOD, d), lambda i, f: (geo.mod_row(i, tm), 0, 0)),
            pl.BlockSpec((4, d), lambda i, f: (0, 0)),
            pl.BlockSpec((None, d, tf), lambda i, f: (lyr, 0, f)),
            pl.BlockSpec((None, d, tf), lambda i, f: (lyr, 0, f)),
            pl.BlockSpec((None, tf, d), lambda i, f: (lyr, f, 0)),
        ],
        out_specs=pl.BlockSpec((tm, d), lambda i, f: (i, 0)),
        out_shape=jax.ShapeDtypeStruct((m, d), F32),
        scratch_shapes=[pltpu.VMEM((tm, d), BF16), pltpu.VMEM((tm, d), F32)],
        compiler_params=_cparams(("arbitrary", "arbitrary")),
        name="ffn",
    )(x, mod, ng, wg, wu, wd)


def _qkv_kernel(hd, scale, x_ref, mod_ref, ng_ref, w_ref, gq_ref, gk_ref, cos_ref, sin_ref,
                q_ref, k_ref, v_ref):
    h = _normmod(x_ref[...], ng_ref[NG_PRE_MIX:NG_PRE_MIX + 1, :],
                 mod_ref[SC1:SC1 + 1, :], mod_ref[SH1:SH1 + 1, :])
    r = _dot(h.astype(BF16), w_ref[...])
    nq = q_ref.shape[1]
    nk = k_ref.shape[1]
    cos = cos_ref[...]
    sin = sin_ref[...]
    for c in range(0, nq, hd):
        qh = _rope(_rms(r[:, c:c + hd]) * gq_ref[...], cos, sin, hd // 4)
        q_ref[:, c:c + hd] = (qh * scale).astype(BF16)
    for c in range(0, nk, hd):
        k_ref[:, c:c + hd] = _rope(_rms(r[:, nq + c:nq + c + hd]) * gk_ref[...], cos, sin, hd // 4)
    v_ref[...] = r[:, nq + nk:]


def _qkv_call(geo, x, mod, ng, wqkv, gq, gk, cos, sin):
    m, d = x.shape
    hd = gq.shape[-1]
    assert hd == LANES
    nq, nk = N_HEADS * hd, N_KV_HEADS * hd
    tm = 512
    geo.check_tile(tm)
    kern = functools.partial(_qkv_kernel, hd, hd ** -0.5)
    return pl.pallas_call(
        kern,
        grid=(m // tm,),
        in_specs=[
            pl.BlockSpec((tm, d), lambda i: (i, 0)),
            pl.BlockSpec((None, N_MOD, d), lambda i: (geo.mod_row(i, tm), 0, 0)),
            pl.BlockSpec((4, d), lambda i: (0, 0)),
            pl.BlockSpec((d, nq + 2 * nk), lambda i: (0, 0)),
            pl.BlockSpec((1, hd), lambda i: (0, 0)),
            pl.BlockSpec((1, hd), lambda i: (0, 0)),
            pl.BlockSpec((tm, LANES), lambda i: (geo.rope_blk(i, tm), 0)),
            pl.BlockSpec((tm, LANES), lambda i: (geo.rope_blk(i, tm), 0)),
        ],
        out_specs=[
            pl.BlockSpec((tm, nq), lambda i: (i, 0)),
            pl.BlockSpec((tm, nk), lambda i: (i, 0)),
            pl.BlockSpec((tm, nk), lambda i: (i, 0)),
        ],
        out_shape=[
            jax.ShapeDtypeStruct((m, nq), BF16),
            jax.ShapeDtypeStruct((m, nk), F32),
            jax.ShapeDtypeStruct((m, nk), F32),
        ],
        compiler_params=_cparams(("arbitrary",)),
        name="qkv",
    )(x, mod, ng, wqkv, gq.reshape(1, hd), gk.reshape(1, hd), cos, sin)


def _attn_kernel(n_group, dk, dv, shared_kv, n_kv_in, *refs):
    q_ref, kv_refs, o_ref, scr = refs[0], refs[1:1 + n_kv_in], refs[1 + n_kv_in], list(refs[2 + n_kv_in:])
    kv = []
    for r in kv_refs:
        if r.dtype == BF16:
            kv.append(r)
        else:
            s = scr.pop(0)
            pl.when(pl.program_id(2) == 0)(functools.partial(_store_bf16, s, r))
            kv.append(s)
    has_cache = n_kv_in == 4

    nt = (((1,), (1,)), ((), ()))
    for g in range(n_group):
        kc = slice(0, dk) if shared_kv else slice(g * dk, (g + 1) * dk)
        vc = slice(0, dv) if shared_kv else slice(g * dv, (g + 1) * dv)
        q = q_ref[:, g * dk:(g + 1) * dk]
        s_n = lax.dot_general(q, kv[0][:, kc], nt, preferred_element_type=F32)
        mx = jnp.max(s_n, axis=-1, keepdims=True)
        if has_cache:
            s_c = lax.dot_general(q, kv[2][:, kc], nt, preferred_element_type=F32)
            mx = jnp.maximum(mx, jnp.max(s_c, axis=-1, keepdims=True))
        p_n = jnp.exp(s_n - mx)
        den = jnp.sum(p_n, axis=-1, keepdims=True)
        o = _dot(p_n.astype(BF16), kv[1][:, vc])
        if has_cache:
            p_c = jnp.exp(s_c - mx)
            den = den + jnp.sum(p_c, axis=-1, keepdims=True)
            o = o + _dot(p_c.astype(BF16), kv[3][:, vc])
        o_ref[:, g * dv:(g + 1) * dv] = (o / den).astype(BF16)


def _store_bf16(dst, src):
    dst[...] = src[...].astype(BF16)


def _attn_call(q, kn, vn, kc, vc, *, n_batch, seq, row0, n_kv, n_group, dk, dv, tq, shared_kv, name):
    has_cache = kc is not None
    assert seq % tq == 0 and row0 % seq == 0
    nqt = seq // tq
    qb0, kb0 = row0 // tq, row0 // seq
    kw, vw = (dk, dv) if shared_kv else (n_group * dk, n_group * dv)
    in_specs = [
        pl.BlockSpec((tq, n_group * dk), lambda b, h, t: (qb0 + b * nqt + t, h)),
        pl.BlockSpec((seq, kw), lambda b, h, t: (kb0 + b, h)),
        pl.BlockSpec((seq, vw), lambda b, h, t: (kb0 + b, h)),
    ]
    args = [q, kn, vn]
    rows = [seq, seq]
    if has_cache:
        past = kc.shape[1]
        in_specs += [
            pl.BlockSpec((None, past, kw), lambda b, h, t: (b, 0, h)),
            pl.BlockSpec((None, past, vw), lambda b, h, t: (b, 0, h)),
        ]
        args += [kc, vc]
        rows += [past, past]
    scratch = [pltpu.VMEM((r, w), BF16)
               for a, r, w in zip(args[1:], rows, (kw, vw, kw, vw)) if a.dtype != BF16]
    kern = functools.partial(_attn_kernel, n_group, dk, dv, shared_kv, len(args) - 1)
    return pl.pallas_call(
        kern,
        grid=(n_batch, n_kv, nqt),
        in_specs=in_specs,
        out_specs=pl.BlockSpec((tq, n_group * dv), lambda b, h, t: (b * nqt + t, h)),
        out_shape=jax.ShapeDtypeStruct((n_batch * seq, n_kv * n_group * dv), BF16),
        scratch_shapes=scratch,
        compiler_params=_cparams(("arbitrary", "arbitrary", "arbitrary")),
        name=name,
    )(*args)


def _attn_out_kernel(npt, x_ref, op_ref, os_ref, mod_ref, ng_ref, wo_ref, y_ref):
    def project(o_ref):
        out = _dot(o_ref[...], wo_ref[...])
        y_ref[...] = x_ref[...] + mod_ref[G1:G1 + 1, :] * (
            _rms(out) * ng_ref[NG_POST_MIX:NG_POST_MIX + 1, :])

    i = pl.program_id(0)
    pl.when(i < npt)(lambda: project(op_ref))
    pl.when(i >= npt)(lambda: project(os_ref))


def _attn_out_call(geo, x, o_p, o_s, mod, ng, wo):
    m, d = x.shape
    tm = 512
    geo.check_tile(tm)
    npt = geo.P // tm
    n = o_p.shape[1]
    return pl.pallas_call(
        functools.partial(_attn_out_kernel, npt),
        grid=(m // tm,),
        in_specs=[
            pl.BlockSpec((tm, d), lambda i: (i, 0)),
            pl.BlockSpec((tm, n), lambda i: (jnp.minimum(i, npt - 1), 0)),
            pl.BlockSpec((tm, n), lambda i: (jnp.maximum(i - npt, 0), 0)),
            pl.BlockSpec((None, N_MOD, d), lambda i: (geo.mod_row(i, tm), 0, 0)),
            pl.BlockSpec((4, d), lambda i: (0, 0)),
            pl.BlockSpec(wo.shape, lambda i: (0, 0)),
        ],
        out_specs=pl.BlockSpec((tm, d), lambda i: (i, 0)),
        out_shape=jax.ShapeDtypeStruct((m, d), F32),
        compiler_params=_cparams(("arbitrary",)),
        name="attn_out",
    )(x, o_p, o_s, mod, ng, wo)


HI_MASK = 0xFFFF0000


def _as_bf16_bits(v):
    return lax.bitcast_convert_type(v.astype(BF16).astype(F32), jnp.uint32)


def _pack_rows(v, out_ref):
    n, d = v.shape
    assert d == 2 * SUBLANES * LANES
    for s in range(SUBLANES):
        lo = _as_bf16_bits(v[:, s * LANES:(s + 1) * LANES])
        hi = _as_bf16_bits(v[:, (s + SUBLANES) * LANES:(s + SUBLANES + 1) * LANES])
        out_ref[pl.ds(s, n, stride=SUBLANES), :] = (lo >> 16) | (hi & jnp.uint32(HI_MASK))


def _unpack_rows(buf, n, s):
    w = buf[pl.ds(s, n, stride=SUBLANES), :]
    lo = lax.bitcast_convert_type(w << 16, F32)
    hi = lax.bitcast_convert_type(w & jnp.uint32(HI_MASK), F32)
    return lo, hi


def _tile_copy(src_hbm, src_row, dst, dst_row, sem):
    return pltpu.make_async_copy(
        src_hbm.at[pl.ds(pl.multiple_of(src_row * SUBLANES, SUBLANES), SUBLANES), :],
        dst.at[pl.ds(pl.multiple_of(dst_row * SUBLANES, SUBLANES), SUBLANES), :], sem)


def _start_gather(idx_ref, src_hbm, dst, n, sem):
    def start(r, c):
        _tile_copy(src_hbm, idx_ref[0, r], dst, r, sem).start()
        return c

    lax.fori_loop(0, n, start, 0, unroll=8)


def _wait_gather(src_hbm, dst, n, sem):
    pltpu.make_async_copy(src_hbm.at[pl.ds(0, n * SUBLANES), :], dst, sem).wait()


def _pipelined_gather(i, n_steps, idx_ref, nxt_ref, src_hbm, bufs, n, sems, consume):
    for sl in (0, 1):
        @pl.when(i % 2 == sl)
        def _(sl=sl):
            if sl == 0:
                @pl.when(i == 0)
                def _():
                    _start_gather(idx_ref, src_hbm, bufs[0], n, sems.at[0])

            @pl.when(i + 1 < n_steps)
            def _():
                _start_gather(nxt_ref, src_hbm, bufs[1 - sl], n, sems.at[1 - sl])

            _wait_gather(src_hbm, bufs[sl], n, sems.at[sl])
            consume(bufs[sl])


def _route_kernel(n_exp, x_ref, mod_ref, ng_ref, wr_ref, br_ref, h_ref, topw_ref, topi_ref):
    h = _normmod(x_ref[...], ng_ref[NG_PRE_FFN:NG_PRE_FFN + 1, :],
                 mod_ref[SC2:SC2 + 1, :], mod_ref[SH2:SH2 + 1, :])
    _pack_rows(h, h_ref)
    logits = jnp.dot(h, wr_ref[...], preferred_element_type=F32,
                     precision=lax.Precision.HIGHEST) + br_ref[...]
    lane = lax.broadcasted_iota(jnp.int32, logits.shape, 1)
    neg = jnp.float32(-jnp.inf)
    logits = jnp.where(lane < n_exp, logits, neg)
    m1 = jnp.max(logits, axis=-1, keepdims=True)
    i1 = jnp.min(jnp.where(logits == m1, lane, LANES), axis=-1, keepdims=True)
    rest = jnp.where(lane == i1, neg, logits)
    m2 = jnp.max(rest, axis=-1, keepdims=True)
    i2 = jnp.min(jnp.where(rest == m2, lane, LANES), axis=-1, keepdims=True)
    e = jnp.exp(m2 - m1)
    w1 = 1.0 / (1.0 + e)
    w2 = e / (1.0 + e)
    topw_ref[...] = jnp.where(lane == 0, w1, jnp.where(lane == 1, w2, 0.0))
    topi_ref[...] = jnp.where(lane == 0, i1, jnp.where(lane == 1, i2, 0))


def _route_call(geo, x, mod, ng, wr_pad, br_pad):
    m, d = x.shape
    tm = 512
    geo.check_tile(tm)
    kern = functools.partial(_route_kernel, N_EXPERTS)
    return pl.pallas_call(
        kern,
        grid=(m // tm,),
        in_specs=[
            pl.BlockSpec((tm, d), lambda i: (i, 0)),
            pl.BlockSpec((None, N_MOD, d), lambda i: (geo.mod_row(i, tm), 0, 0)),
            pl.BlockSpec((4, d), lambda i: (0, 0)),
            pl.BlockSpec((d, LANES), lambda i: (0, 0)),
            pl.BlockSpec((1, LANES), lambda i: (0, 0)),
        ],
        out_specs=[
            pl.BlockSpec((tm * SUBLANES, LANES), lambda i: (i, 0)),
            pl.BlockSpec((tm, LANES), lambda i: (i, 0)),
            pl.BlockSpec((tm, LANES), lambda i: (i, 0)),
        ],
        out_shape=[
            jax.ShapeDtypeStruct((m * SUBLANES, LANES), jnp.uint32),
            jax.ShapeDtypeStruct((m, LANES), F32),
            jax.ShapeDtypeStruct((m, LANES), jnp.int32),
        ],
        compiler_params=_cparams(("arbitrary",)),
        name="route",
    )(x, mod, ng, wr_pad, br_pad)


def _dispatch(topi, topw, tr):
    m = topi.shape[0]
    n_asg = TOP_K * m
    n_tiles = n_asg // tr + N_EXPERTS
    e = topi[:, :TOP_K].reshape(n_asg)
    w = topw[:, :TOP_K].reshape(n_asg)
    onehot = (e[:, None] == jnp.arange(N_EXPERTS, dtype=jnp.int32)[None, :]).astype(jnp.int32)
    csum = jnp.cumsum(onehot, axis=0)
    count = csum[-1]
    rank = jnp.sum(csum * onehot, axis=1) - 1
    ptiles = (count + tr - 1) // tr
    tile_end = jnp.cumsum(ptiles)
    tile_start = tile_end - ptiles
    n_used = tile_end[-1:]
    dest = tile_start[e] * tr + rank
    asg = jnp.full((n_tiles * tr,), -1, jnp.int32).at[dest].set(jnp.arange(n_asg, dtype=jnp.int32))
    src = jnp.maximum(asg, 0) // TOP_K
    gate = jnp.where(asg >= 0, w[jnp.maximum(asg, 0)], 0.0)
    tile_expert = jnp.minimum(
        jnp.sum((jnp.arange(n_tiles, dtype=jnp.int32)[:, None] >= tile_end[None, :]).astype(jnp.int32), axis=1),
        N_EXPERTS - 1)
    return src, gate, tile_expert, n_used.astype(jnp.int32), dest


def _moe_kernel(te_ref, nu_ref, src_ref, nxt_ref, h_hbm, g_ref, wg_ref, wu_ref, wd_ref, y_ref,
                buf0, buf1, hbf, acc, sems):
    i = pl.program_id(0)
    f = pl.program_id(1)
    n_used = nu_ref[0]
    used = i < n_used
    tr = hbf.shape[0]

    def unpack(buf):
        for s in range(SUBLANES):
            lo, hi = _unpack_rows(buf, tr, s)
            hbf[:, s * LANES:(s + 1) * LANES] = lo.astype(BF16)
            hbf[:, (s + SUBLANES) * LANES:(s + SUBLANES + 1) * LANES] = hi.astype(BF16)

    @pl.when(used & (f == 0))
    def _():
        _pipelined_gather(i, n_used, src_ref, nxt_ref, h_hbm, (buf0, buf1), tr, sems, unpack)
        acc[...] = jnp.zeros_like(acc)

    @pl.when(used)
    def _():
        h = hbf[...]
        g = g_ref[...]
        tf = wg_ref.shape[1]
        for c0 in range(0, tf, MOE_TC):
            c1 = min(c0 + MOE_TC, tf)
            t = _silu(_dot(h, wg_ref[:, c0:c1])) * _dot(h, wu_ref[:, c0:c1]) * g
            acc[...] += _dot(t.astype(BF16), wd_ref[c0:c1, :])

    last = f == pl.num_programs(1) - 1

    @pl.when(used & last)
    def _():
        _pack_rows(acc[...], y_ref)

    @pl.when(jnp.logical_not(used) & last)
    def _():
        y_ref[...] = jnp.zeros_like(y_ref)


def _moe_call(h, src, gate, tile_expert, n_used, wg, wu, wd, lyr, tr):
    _, n_exp, d, dfe = wg.shape
    n_tiles = tile_expert.shape[0]
    tf = MOE_TF
    assert dfe % tf == 0
    nf = dfe // tf

    def fblk(i, f, nu):
        return jnp.where(i < nu[0], f, nf - 1)

    src = src.reshape(n_tiles, 1, tr)
    tile_rows = tr * SUBLANES
    grid_spec = pltpu.PrefetchScalarGridSpec(
        num_scalar_prefetch=2,
        grid=(n_tiles, nf),
        in_specs=[
            pl.BlockSpec((None, 1, tr), lambda i, f, te, nu: (i, 0, 0), memory_space=pltpu.SMEM),
            pl.BlockSpec((None, 1, tr), lambda i, f, te, nu: (jnp.minimum(i + 1, n_tiles - 1), 0, 0),
                         memory_space=pltpu.SMEM),
            pl.BlockSpec(memory_space=pl.ANY),
            pl.BlockSpec((tr, 1), lambda i, f, te, nu: (i, 0)),
            pl.BlockSpec((None, None, d, tf), lambda i, f, te, nu: (lyr, te[i], 0, fblk(i, f, nu))),
            pl.BlockSpec((None, None, d, tf), lambda i, f, te, nu: (lyr, te[i], 0, fblk(i, f, nu))),
            pl.BlockSpec((None, None, tf, d), lambda i, f, te, nu: (lyr, te[i], fblk(i, f, nu), 0)),
        ],
        out_specs=pl.BlockSpec((tile_rows, LANES), lambda i, f, te, nu: (i, 0)),
        scratch_shapes=[pltpu.VMEM((tile_rows, LANES), jnp.uint32), pltpu.VMEM((tile_rows, LANES), jnp.uint32),
                        pltpu.VMEM((tr, d), BF16), pltpu.VMEM((tr, d), F32),
                        pltpu.SemaphoreType.DMA((2,))],
    )
    return pl.pallas_call(
        _moe_kernel,
        grid_spec=grid_spec,
        out_shape=jax.ShapeDtypeStruct((n_tiles * tile_rows, LANES), jnp.uint32),
        compiler_params=_cparams(("arbitrary", "arbitrary")),
        name="moe",
    )(tile_expert, n_used, src, src, h, gate.reshape(n_tiles * tr, 1), wg, wu, wd)


def _combine_kernel(npt, pos_ref, nxt_ref, y_hbm, x_ref, mod_ref, ng_ref, *refs):
    outs, (buf0, buf1, ysum, sems) = refs[:-4], refs[-4:]
    i = pl.program_id(0)
    tm = x_ref.shape[0]

    def add_pairs(buf):
        for s in range(SUBLANES):
            lo, hi = _unpack_rows(buf, TOP_K * tm, s)
            ysum[:, s * LANES:(s + 1) * LANES] = lo[:tm] + lo[tm:]
            ysum[:, (s + SUBLANES) * LANES:(s + SUBLANES + 1) * LANES] = hi[:tm] + hi[tm:]

    _pipelined_gather(i, pl.num_programs(0), pos_ref, nxt_ref, y_hbm, (buf0, buf1), TOP_K * tm, sems,
                      add_pairs)
    res = x_ref[...] + mod_ref[G2:G2 + 1, :] * (
        _rms(ysum[...]) * ng_ref[NG_POST_FFN:NG_POST_FFN + 1, :])
    if npt is None:
        outs[0][...] = res
    else:
        @pl.when(i < npt)
        def _():
            outs[0][...] = res

        @pl.when(i >= npt)
        def _():
            outs[1][...] = res


def _combine_call(geo, x, y, dest, mod, ng, split):
    m, d = x.shape
    tm = 256
    geo.check_tile(tm)
    nt = m // tm
    npt = geo.P // tm
    pos = dest.reshape(nt, tm, TOP_K).transpose(0, 2, 1).reshape(nt, 1, TOP_K * tm)
    if split:
        out_specs = [pl.BlockSpec((tm, d), lambda i: (jnp.minimum(i, npt - 1), 0)),
                     pl.BlockSpec((tm, d), lambda i: (jnp.maximum(i - npt, 0), 0))]
        out_shape = [jax.ShapeDtypeStruct((geo.P, d), F32), jax.ShapeDtypeStruct((m - geo.P, d), F32)]
    else:
        out_specs = pl.BlockSpec((tm, d), lambda i: (i, 0))
        out_shape = jax.ShapeDtypeStruct((m, d), F32)
    buf = pltpu.VMEM((TOP_K * tm * SUBLANES, LANES), jnp.uint32)
    return pl.pallas_call(
        functools.partial(_combine_kernel, npt if split else None),
        grid=(nt,),
        in_specs=[
            pl.BlockSpec((None, 1, TOP_K * tm), lambda i: (i, 0, 0), memory_space=pltpu.SMEM),
            pl.BlockSpec((None, 1, TOP_K * tm), lambda i: (jnp.minimum(i + 1, nt - 1), 0, 0),
                         memory_space=pltpu.SMEM),
            pl.BlockSpec(memory_space=pl.ANY),
            pl.BlockSpec((tm, d), lambda i: (i, 0)),
            pl.BlockSpec((None, N_MOD, d), lambda i: (geo.mod_row(i, tm), 0, 0)),
            pl.BlockSpec((4, d), lambda i: (0, 0)),
        ],
        out_specs=out_specs,
        out_shape=out_shape,
        scratch_shapes=[buf, buf, pltpu.VMEM((tm, d), F32), pltpu.SemaphoreType.DMA((2,))],
        compiler_params=_cparams(("arbitrary",)),
        name="moe_combine",
    )(pos, pos, y, x, mod, ng)


def _mla_in_kernel(scale, x_ref, mod_ref, ng_ref, wd_ref, gq_ref, gkv_ref, wuq_ref, wuk_ref, wuv_ref,
                   cos_ref, sin_ref, q_ref, kcat_ref, v_ref, ckv_ref, kpe_ref):
    h = _normmod(x_ref[...], ng_ref[NG_PRE_MIX:NG_PRE_MIX + 1, :],
                 mod_ref[SC1:SC1 + 1, :], mod_ref[SH1:SH1 + 1, :])
    r = _dot(h.astype(BF16), wd_ref[...])
    qr = gq_ref.shape[1]
    kr = gkv_ref.shape[1]
    rope_dim = kpe_ref.shape[1]
    cos = cos_ref[...]
    sin = sin_ref[...]
    cq = (_rms(r[:, :qr]) * gq_ref[...]).astype(BF16)
    ckv = _rms(r[:, qr:qr + kr]) * gkv_ref[...]
    kpe = r[:, qr + kr:qr + kr + LANES]
    ckv_ref[...] = ckv
    kpe_ref[...] = kpe[:, :rope_dim]
    ckv_b = ckv.astype(BF16)
    q = _dot(cq, wuq_ref[...])
    kn = _dot(ckv_b, wuk_ref[...])
    v_ref[...] = _dot(ckv_b, wuv_ref[...]).astype(BF16)
    kpe_r = _rope(kpe, cos, sin, rope_dim // 4).astype(BF16)
    for hh in range(q_ref.shape[1] // MLA_QK):
        c = hh * MLA_QK
        q_ref[:, c:c + NOPE_DIM] = (q[:, c:c + NOPE_DIM] * scale).astype(BF16)
        q_ref[:, c + NOPE_DIM:c + MLA_QK] = (
            _rope(q[:, c + NOPE_DIM:c + MLA_QK], cos, sin, rope_dim // 4) * scale).astype(BF16)
        kcat_ref[:, c:c + NOPE_DIM] = kn[:, hh * NOPE_DIM:(hh + 1) * NOPE_DIM].astype(BF16)
        kcat_ref[:, c + NOPE_DIM:c + MLA_QK] = kpe_r


def _mla_in_call(geo, x, mod, ng, wd_cat, gq, gkv, wuq_pad, wuk, wuv, cos, sin):
    m, d = x.shape
    qr, kr = gq.shape[-1], gkv.shape[-1]
    nqk = wuq_pad.shape[1]
    nv = wuv.shape[1]
    tm = 512
    geo.check_tile(tm)
    kern = functools.partial(_mla_in_kernel, (NOPE_DIM + ROPE_DIM) ** -0.5)
    full = lambda a: pl.BlockSpec(a.shape, lambda i: (0, 0))
    row = lambda n: pl.BlockSpec((tm, n), lambda i: (i, 0))
    return pl.pallas_call(
        kern,
        grid=(m // tm,),
        in_specs=[
            row(d),
            pl.BlockSpec((None, N_MOD, d), lambda i: (geo.mod_row(i, tm), 0, 0)),
            pl.BlockSpec((4, d), lambda i: (0, 0)),
            full(wd_cat),
            pl.BlockSpec((1, qr), lambda i: (0, 0)),
            pl.BlockSpec((1, kr), lambda i: (0, 0)),
            full(wuq_pad), full(wuk), full(wuv),
            pl.BlockSpec((tm, LANES), lambda i: (geo.rope_blk(i, tm), 0)),
            pl.BlockSpec((tm, LANES), lambda i: (geo.rope_blk(i, tm), 0)),
        ],
        out_specs=[row(nqk), row(nqk), row(nv), row(kr), row(ROPE_DIM)],
        out_shape=[
            jax.ShapeDtypeStruct((m, nqk), BF16),
            jax.ShapeDtypeStruct((m, nqk), BF16),
            jax.ShapeDtypeStruct((m, nv), BF16),
            jax.ShapeDtypeStruct((m, kr), F32),
            jax.ShapeDtypeStruct((m, ROPE_DIM), F32),
        ],
        compiler_params=_cparams(("arbitrary",)),
        name="mla_in",
    )(x, mod, ng, wd_cat, gq.reshape(1, qr), gkv.reshape(1, kr), wuq_pad, wuk, wuv, cos, sin)


def _mla_cache_kernel(ckv_ref, kpe_ref, wuk_ref, wuv_ref, kcat_ref, v_ref):
    ckv = ckv_ref[...].astype(BF16)
    kn = _dot(ckv, wuk_ref[...])
    v_ref[...] = _dot(ckv, wuv_ref[...]).astype(BF16)
    kpe = kpe_ref[...].astype(BF16)
    for hh in range(kcat_ref.shape[1] // MLA_QK):
        c = hh * MLA_QK
        kcat_ref[:, c:c + NOPE_DIM] = kn[:, hh * NOPE_DIM:(hh + 1) * NOPE_DIM].astype(BF16)
        kcat_ref[:, c + NOPE_DIM:c + MLA_QK] = kpe


def _mla_cache_call(ckv, kpe_pad, wuk, wuv):
    n, kr = ckv.shape
    tm = 512
    assert n % tm == 0
    nqk = MLA_HEADS * MLA_QK
    nv = wuv.shape[1]
    return pl.pallas_call(
        _mla_cache_kernel,
        grid=(n // tm,),
        in_specs=[
            pl.BlockSpec((tm, kr), lambda i: (i, 0)),
            pl.BlockSpec((tm, LANES), lambda i: (i, 0)),
            pl.BlockSpec(wuk.shape, lambda i: (0, 0)),
            pl.BlockSpec(wuv.shape, lambda i: (0, 0)),
        ],
        out_specs=[pl.BlockSpec((tm, nqk), lambda i: (i, 0)), pl.BlockSpec((tm, nv), lambda i: (i, 0))],
        out_shape=[jax.ShapeDtypeStruct((n, nqk), BF16), jax.ShapeDtypeStruct((n, nv), BF16)],
        compiler_params=_cparams(("arbitrary",)),
        name="mla_cache",
    )(ckv, kpe_pad, wuk, wuv)


def _rope_table(n_tokens, dim, n_ident):
    rows = n_tokens // GRID_W
    row = jnp.repeat(jnp.arange(rows, dtype=F32), GRID_W)
    col = jnp.tile(jnp.arange(GRID_W, dtype=F32), rows)
    half = dim // 2
    inv = ROPE_THETA ** (-jnp.arange(0, half, 2, dtype=F32) / half)
    ar = row[:, None] * inv[None, :]
    ac = col[:, None] * inv[None, :]
    ang = jnp.concatenate([ar, ar, ac, ac], axis=-1)
    cos = jnp.cos(ang)
    sin = jnp.sin(ang)
    hb = dim // 4
    first = (jnp.arange(dim) % (2 * hb)) < hb
    sin = jnp.where(first[None, :], -sin, sin)
    cos = jnp.pad(cos, ((0, n_ident), (0, LANES - dim)), constant_values=1.0)
    sin = jnp.pad(sin, ((0, n_ident), (0, LANES - dim)))
    return cos, sin


def kernel(x_prompt, x_sample, cache_gqa_k, cache_gqa_v, cache_mla_ckv, cache_mla_kpe, c, c_ctx,
           w_mod, b_mod, norm_g,
           conv_w1, conv_b1, conv_wdw, conv_bdw, conv_ln_g, conv_ln_b, conv_w2, conv_b2,
           gqa_wq, gqa_wk, gqa_wv, gqa_gq, gqa_gk, gqa_wo,
           mla_wdq, mla_gq, mla_wuq, mla_wdkv, mla_gkv, mla_wuk, mla_wuv, mla_wo,
           ffn_wg, ffn_wu, ffn_wd,
           moe_wr, moe_br, moe_wg, moe_wu, moe_wd):
    b, s, d = x_prompt.shape
    db, t, _ = x_sample.shape
    geo = _Geom(b, s, db, t)
    depth = w_mod.shape[0]
    past = cache_gqa_k.shape[2]
    hd = gqa_gq.shape[-1]
    bf = lambda a: a.astype(BF16)

    assert N_MIXERS == 3
    x = (x_prompt.reshape(geo.P, d), x_sample.reshape(db * t, d))

    n_cond = 1 + db
    cond = jnp.concatenate([c_ctx[None, :], c], axis=0)
    cond = jnp.pad(cond, ((0, -n_cond % 8), (0, 0)))
    mods = _mod_call(cond, w_mod, b_mod).reshape(depth, cond.shape[0], N_MOD, d)

    rope_tm = 512
    cos_hd, sin_hd = _rope_table(t, hd, rope_tm)
    cos_pe, sin_pe = _rope_table(t, ROPE_DIM, rope_tm)

    conv_w1_b, conv_w2_b = bf(conv_w1), bf(conv_w2)
    ffn_wg_b, ffn_wu_b, ffn_wd_b = bf(ffn_wg), bf(ffn_wu), bf(ffn_wd)
    moe_wg_b, moe_wu_b, moe_wd_b = bf(moe_wg), bf(moe_wu), bf(moe_wd)

    ks, vs, cs, ps = [], [], [], []
    for i in range(depth):
        mod, ng = mods[i], norm_g[i]
        j = i // N_MIXERS
        if i % N_MIXERS == 0:
            u = _conv_in_call(geo, x, mod, ng, conv_w1_b, j, conv_b1[j])
            x = _conv_out_call(geo, u, x, mod, ng, conv_wdw[j], conv_bdw[j], conv_ln_g[j],
                               conv_ln_b[j], conv_w2_b, j, conv_b2[j])
        elif i % N_MIXERS == 1:
            wqkv = bf(jnp.concatenate([gqa_wq[j], gqa_wk[j], gqa_wv[j]], axis=1))
            q, k, v = _qkv_call(geo, x, mod, ng, wqkv, gqa_gq[j], gqa_gk[j], cos_hd, sin_hd)
            ks.append(k[:geo.P].reshape(b, s, N_KV_HEADS, hd))
            vs.append(v[:geo.P].reshape(b, s, N_KV_HEADS, hd))
            kw = dict(n_kv=N_KV_HEADS, n_group=N_HEADS // N_KV_HEADS, dk=hd, dv=hd, shared_kv=True)
            o_p = _attn_call(q, k, v, None, None, n_batch=b, seq=s, row0=0, tq=min(s, 256),
                             name="gqa_attn_prompt", **kw)
            o_s = _attn_call(q, k, v,
                             cache_gqa_k[:, j].reshape(db, past, N_KV_HEADS * hd),
                             cache_gqa_v[:, j].reshape(db, past, N_KV_HEADS * hd),
                             n_batch=db, seq=t, row0=geo.P, tq=256, name="gqa_attn_sample", **kw)
            x = _attn_out_call(geo, x, o_p, o_s, mod, ng, bf(gqa_wo[j]))
        else:
            qr, kr = mla_gq.shape[-1], mla_gkv.shape[-1]
            wd_cat = bf(jnp.pad(jnp.concatenate([mla_wdq[j], mla_wdkv[j]], axis=1),
                                ((0, 0), (0, LANES - ROPE_DIM))))
            wuq_pad = bf(jnp.pad(mla_wuq[j], ((0, 0), (0, 0), (0, MLA_QK - NOPE_DIM - ROPE_DIM)))
                         ).reshape(qr, MLA_HEADS * MLA_QK)
            wuk = bf(mla_wuk[j]).reshape(kr, MLA_HEADS * NOPE_DIM)
            wuv = bf(mla_wuv[j]).reshape(kr, MLA_HEADS * V_DIM)
            q, kcat, v, ckv, kpe = _mla_in_call(geo, x, mod, ng, wd_cat, mla_gq[j], mla_gkv[j],
                                                wuq_pad, wuk, wuv, cos_pe, sin_pe)
            cs.append(ckv[:geo.P].reshape(b, s, kr))
            ps.append(kpe[:geo.P].reshape(b, s, ROPE_DIM))
            kc, vc = _mla_cache_call(
                cache_mla_ckv[:, j].reshape(db * past, kr),
                jnp.pad(cache_mla_kpe[:, j].reshape(db * past, ROPE_DIM), ((0, 0), (0, LANES - ROPE_DIM))),
                wuk, wuv)
            kw = dict(n_kv=MLA_HEADS // MLA_HPS, n_group=MLA_HPS, dk=MLA_QK, dv=V_DIM, shared_kv=False)
            o_p = _attn_call(q, kcat, v, None, None, n_batch=b, seq=s, row0=0, tq=min(s, 256),
                             name="mla_attn_prompt", **kw)
            o_s = _attn_call(q, kcat, v, kc.reshape(db, past, -1), vc.reshape(db, past, -1),
                             n_batch=db, seq=t, row0=geo.P, tq=256, name="mla_attn_sample", **kw)
            x = _attn_out_call(geo, x, o_p, o_s, mod, ng, bf(mla_wo[j]))
        f = i // 2
        if i % 2 == 0:
            x = _ffn_call(geo, x, mod, ng, ffn_wg_b, ffn_wu_b, ffn_wd_b, f)
        else:
            wr_pad = jnp.pad(moe_wr[f], ((0, 0), (0, LANES - N_EXPERTS)))
            br_pad = jnp.pad(moe_br[f], (0, LANES - N_EXPERTS)).reshape(1, LANES)
            h, topw, topi = _route_call(geo, x, mod, ng, wr_pad, br_pad)
            src, gate, tile_expert, n_used, dest = _dispatch(topi, topw, MOE_TR)
            y = _moe_call(h, src, gate, tile_expert, n_used, moe_wg_b, moe_wu_b, moe_wd_b, f, MOE_TR)
            x = _combine_call(geo, x, y, dest, mod, ng, split=(i == depth - 1))

    y_prompt, y_sample = x if isinstance(x, (list, tuple)) else (x[:geo.P], x[geo.P:])
    return (y_prompt.reshape(b, s, d), y_sample.reshape(db, t, d), jnp.stack(ks, axis=1), jnp.stack(vs, axis=1),
            jnp.stack(cs, axis=1), jnp.stack(ps, axis=1))
```

```python
import functools

import jax
import jax.numpy as jnp
from jax import lax
from jax.experimental import pallas as pl
from jax.experimental.pallas import tpu as pltpu

F32 = jnp.float32
BF16 = jnp.bfloat16

EPS = 1e-6
GRID_W = 64
ROPE_THETA = 10000.0
N_MIXERS = 3
N_MOD = 6
CONV_W = 31
CONV_PAD = CONV_W // 2
N_HEADS = 16
N_KV_HEADS = 4
MLA_HEADS = 16
NOPE_DIM = 128
ROPE_DIM = 64
V_DIM = 128
N_EXPERTS = 8
TOP_K = 2
MOE_TR = 512
MOE_TF = 1408
MOE_TC = 256

LANES = 128
SUBLANES = 8
HALO = 16
MLA_QK = 256
MLA_HPS = 4
VMEM_LIMIT = 56 * 1024 * 1024

SH1, SC1, G1, SH2, SC2, G2 = range(6)
NG_PRE_MIX, NG_POST_MIX, NG_PRE_FFN, NG_POST_FFN = range(4)


def _cparams(sem):
    return pltpu.CompilerParams(dimension_semantics=sem, vmem_limit_bytes=VMEM_LIMIT)


def _rms(x):
    return x * lax.rsqrt(jnp.mean(x * x, axis=-1, keepdims=True) + EPS)


def _normmod(x, g, sc, sh):
    return (_rms(x) * g) * (1.0 + sc) + sh


def _silu(x):
    return x * jax.nn.sigmoid(x)


def _dot(a, b):
    return jnp.dot(a, b, preferred_element_type=F32)


def _half_tiles(tm):
    half = tm // 2
    return (slice(0, half), slice(half, tm))


def _rope(x, cos, sin_signed, hb):
    lane = lax.broadcasted_iota(jnp.int32, x.shape, 1)
    first = (lane % (2 * hb)) < hb
    rot = jnp.where(first, pltpu.roll(x, LANES - hb, 1), pltpu.roll(x, hb, 1))
    return x * cos + rot * sin_signed


class _Geom:
    def __init__(self, b, s, db, t):
        self.B, self.S, self.DB, self.T = b, s, db, t
        self.P = b * s
        self.M = self.P + db * t

    def check_tile(self, tm):
        assert self.P % tm == 0 and self.T % tm == 0, (self.P, self.T, tm)

    def mod_row(self, i, tm):
        npt = self.P // tm
        return jnp.where(i < npt, 0, 1 + (i - npt) // (self.T // tm))

    def rope_blk(self, i, tm):
        npt = self.P // tm
        return jnp.where(i < npt, self.T // tm, (i - npt) % (self.T // tm))


def _mod_kernel(c_ref, w_ref, b_ref, o_ref):
    c = c_ref[...]
    o_ref[...] = _dot(_silu(c).astype(BF16), w_ref[...].astype(BF16)) + b_ref[...]


def _mod_call(cond, w_mod, b_mod):
    depth, d, n = w_mod.shape
    rows = cond.shape[0]
    tn = 1024
    return pl.pallas_call(
        _mod_kernel,
        grid=(depth, n // tn),
        in_specs=[
            pl.BlockSpec((rows, d), lambda l, j: (0, 0)),
            pl.BlockSpec((None, d, tn), lambda l, j: (l, 0, j)),
            pl.BlockSpec((None, 1, tn), lambda l, j: (l, 0, j)),
        ],
        out_specs=pl.BlockSpec((None, rows, tn), lambda l, j: (l, 0, j)),
        out_shape=jax.ShapeDtypeStruct((depth, rows, n), F32),
        compiler_params=_cparams(("arbitrary", "arbitrary")),
        name="mod",
    )(cond, w_mod, b_mod.reshape(depth, 1, n))


def _x_specs(geo, x, tm, n_grid):
    def spec(f):
        return pl.BlockSpec((tm, x[0].shape[1] if isinstance(x, tuple) else x.shape[1]),
                            (lambda i: (f(i), 0)) if n_grid == 1 else (lambda i, j: (f(i), 0)))

    if not isinstance(x, tuple):
        return (x,), [spec(lambda i: i)]
    npt = geo.P // tm
    return x, [spec(lambda i: jnp.minimum(i, npt - 1)), spec(lambda i: jnp.maximum(i - npt, 0))]


def _with_x_tile(i, npt, x_refs, fn):
    if len(x_refs) == 1:
        fn(x_refs[0])
    else:
        pl.when(i < npt)(lambda: fn(x_refs[0]))
        pl.when(i >= npt)(lambda: fn(x_refs[1]))


def _conv_in_kernel(n_x, npt, *refs):
    x_refs = refs[:n_x]
    mod_ref, ng_ref, wa_ref, wg_ref, ba_ref, bg_ref, u_ref, h_scr = refs[n_x:]

    def prologue(x_ref):
        h = _normmod(x_ref[...], ng_ref[NG_PRE_MIX:NG_PRE_MIX + 1, :],
                     mod_ref[SC1:SC1 + 1, :], mod_ref[SH1:SH1 + 1, :])
        h_scr[...] = h.astype(BF16)

    @pl.when(pl.program_id(1) == 0)
    def _():
        _with_x_tile(pl.program_id(0), npt, x_refs, prologue)

    h = h_scr[...]
    a = _dot(h, wa_ref[...]) + ba_ref[...]
    g = _dot(h, wg_ref[...]) + bg_ref[...]
    u_ref[...] = a * jax.nn.sigmoid(g)


def _conv_in_call(geo, x, mod, ng, w1, lyr, b1):
    m, d = geo.M, w1.shape[1]
    tm, tn = 512, 1024
    geo.check_tile(tm)
    nj = d // tn
    b1 = b1.reshape(1, 2 * d)
    xs, x_specs = _x_specs(geo, x, tm, 2)
    return pl.pallas_call(
        functools.partial(_conv_in_kernel, len(xs), geo.P // tm),
        grid=(m // tm, nj),
        in_specs=x_specs + [
            pl.BlockSpec((None, N_MOD, d), lambda i, j: (geo.mod_row(i, tm), 0, 0)),
            pl.BlockSpec((4, d), lambda i, j: (0, 0)),
            pl.BlockSpec((None, d, tn), lambda i, j: (lyr, 0, j)),
            pl.BlockSpec((None, d, tn), lambda i, j: (lyr, 0, j + nj)),
            pl.BlockSpec((1, tn), lambda i, j: (0, j)),
            pl.BlockSpec((1, tn), lambda i, j: (0, j + nj)),
        ],
        out_specs=pl.BlockSpec((tm, tn), lambda i, j: (i, j)),
        out_shape=jax.ShapeDtypeStruct((m, d), F32),
        scratch_shapes=[pltpu.VMEM((tm, d), BF16)],
        compiler_params=_cparams(("arbitrary", "arbitrary")),
        name="conv_in",
    )(*xs, mod, ng, w1, w1, b1, b1)


CONV_TM = 256
CONV_RC = 32
CONV_CC = 256


def _conv_out_kernel(n_x, tiles_p, tps_p, tps_s, *refs):
    x_refs = refs[:n_x]
    (u_ref, up_ref, un_ref, mod_ref, ng_ref, wdw_ref, bdw_ref,
     lng_ref, lnb_ref, w2_ref, b2_ref, o_ref, ext, cv, sh) = refs[n_x:]
    i = pl.program_id(0)
    is_p = i < tiles_p
    j = jnp.where(is_p, i % tps_p, (i - tiles_p) % tps_s)
    n = jnp.where(is_p, tps_p, tps_s)
    tm, d = u_ref.shape
    ext[HALO:HALO + tm, :] = u_ref[...]
    ext[0:HALO, :] = jnp.where(j == 0, 0.0, up_ref[...])
    ext[HALO + tm:HALO + tm + HALO, :] = jnp.where(j == n - 1, 0.0, un_ref[...])

    def chunk(c, carry):
        c0 = pl.multiple_of(c * CONV_CC, CONV_CC)
        e = ext[:, pl.ds(c0, CONV_CC)]
        for p in range(1, SUBLANES):
            sh[p - 1] = pltpu.roll(e, e.shape[0] - p, 0)
        for r in range(0, tm, CONV_RC):
            acc = jnp.zeros((CONV_RC, CONV_CC), F32)
            for k in range(CONV_W):
                r0 = HALO - CONV_PAD + r + k
                p = r0 % SUBLANES
                a0 = r0 - p
                if p == 0:
                    tap = ext[a0:a0 + CONV_RC, pl.ds(c0, CONV_CC)]
                else:
                    tap = sh[p - 1, a0:a0 + CONV_RC, :]
                acc = acc + tap * wdw_ref[k:k + 1, pl.ds(c0, CONV_CC)]
            cv[r:r + CONV_RC, pl.ds(c0, CONV_CC)] = acc
        return carry

    lax.fori_loop(0, d // CONV_CC, chunk, 0)

    v = cv[...] + bdw_ref[...]
    mu = jnp.mean(v, axis=-1, keepdims=True)
    xc = v - mu
    var = jnp.mean(xc * xc, axis=-1, keepdims=True)
    y = _silu(xc * lax.rsqrt(var + EPS) * lng_ref[...] + lnb_ref[...])
    out = _dot(y.astype(BF16), w2_ref[...]) + b2_ref[...]
    upd = mod_ref[G1:G1 + 1, :] * (_rms(out) * ng_ref[NG_POST_MIX:NG_POST_MIX + 1, :])

    def residual(x_ref):
        o_ref[...] = x_ref[...] + upd

    _with_x_tile(i, tiles_p, x_refs, residual)


def _conv_out_call(geo, u, x, mod, ng, wdw, bdw, lng, lnb, w2, lyr, b2):
    m, d = u.shape
    tm = CONV_TM
    assert geo.S % tm == 0 and geo.T % tm == 0
    hb = tm // HALO
    nhb = m // HALO
    xs, x_specs = _x_specs(geo, x, tm, 1)
    kern = functools.partial(_conv_out_kernel, len(xs), geo.P // tm, geo.S // tm, geo.T // tm)
    vec = lambda a: a.reshape(1, d)
    return pl.pallas_call(
        kern,
        grid=(m // tm,),
        in_specs=x_specs + [
            pl.BlockSpec((tm, d), lambda i: (i, 0)),
            pl.BlockSpec((HALO, d), lambda i: (jnp.maximum(i * hb - 1, 0), 0)),
            pl.BlockSpec((HALO, d), lambda i: (jnp.minimum((i + 1) * hb, nhb - 1), 0)),
            pl.BlockSpec((None, N_MOD, d), lambda i: (geo.mod_row(i, tm), 0, 0)),
            pl.BlockSpec((4, d), lambda i: (0, 0)),
            pl.BlockSpec((CONV_W, d), lambda i: (0, 0)),
            pl.BlockSpec((1, d), lambda i: (0, 0)),
            pl.BlockSpec((1, d), lambda i: (0, 0)),
            pl.BlockSpec((1, d), lambda i: (0, 0)),
            pl.BlockSpec((None, d, d), lambda i: (lyr, 0, 0)),
            pl.BlockSpec((1, d), lambda i: (0, 0)),
        ],
        out_specs=pl.BlockSpec((tm, d), lambda i: (i, 0)),
        out_shape=jax.ShapeDtypeStruct((m, d), F32),
        scratch_shapes=[pltpu.VMEM((tm + 2 * HALO, d), F32), pltpu.VMEM((tm, d), F32),
                        pltpu.VMEM((SUBLANES - 1, tm + 2 * HALO, CONV_CC), F32)],
        compiler_params=_cparams(("arbitrary",)),
        name="conv_out",
    )(*xs, u, u, u, mod, ng, wdw, vec(bdw), vec(lng), vec(lnb), w2, vec(b2))


def _ffn_kernel(x_ref, mod_ref, ng_ref, wg_ref, wu_ref, wd_ref, o_ref, h_scr, acc):
    f = pl.program_id(1)

    @pl.when(f == 0)
    def _():
        h = _normmod(x_ref[...], ng_ref[NG_PRE_FFN:NG_PRE_FFN + 1, :],
                     mod_ref[SC2:SC2 + 1, :], mod_ref[SH2:SH2 + 1, :])
        h_scr[...] = h.astype(BF16)
        acc[...] = jnp.zeros_like(acc)

    h = h_scr[...]
    t = _silu(_dot(h, wg_ref[...])) * _dot(h, wu_ref[...])
    acc[...] += _dot(t.astype(BF16), wd_ref[...])

    @pl.when(f == pl.num_programs(1) - 1)
    def _():
        o_ref[...] = x_ref[...] + mod_ref[G2:G2 + 1, :] * (
            _rms(acc[...]) * ng_ref[NG_POST_FFN:NG_POST_FFN + 1, :])


def _ffn_call(geo, x, mod, ng, wg, wu, wd, lyr):
    m, d = x.shape
    dff = wg.shape[2]
    tm, tf = 512, 512
    geo.check_tile(tm)
    return pl.pallas_call(
        _ffn_kernel,
        grid=(m // tm, dff // tf),
        in_specs=[
            pl.BlockSpec((tm, d), lambda i, f: (i, 0)),
            pl.BlockSpec((None, N_MOD, d), lambda i, f: (geo.mod_row(i, tm), 0, 0)),
            pl.BlockSpec((4, d), lambda i, f: (0, 0)),
            pl.BlockSpec((None, d, tf), lambda i, f: (lyr, 0, f)),
            pl.BlockSpec((None, d, tf), lambda i, f: (lyr, 0, f)),
            pl.BlockSpec((None, tf, d), lambda i, f: (lyr, f, 0)),
        ],
        out_specs=pl.BlockSpec((tm, d), lambda i, f: (i, 0)),
        out_shape=jax.ShapeDtypeStruct((m, d), F32),
        scratch_shapes=[pltpu.VMEM((tm, d), BF16), pltpu.VMEM((tm, d), F32)],
        compiler_params=_cparams(("arbitrary", "arbitrary")),
        name="ffn",
    )(x, mod, ng, wg, wu, wd)


def _qkv_kernel(hd, scale, x_ref, mod_ref, ng_ref, w_ref, gq_ref, gk_ref, cos_ref, sin_ref,
                q_ref, k_ref, v_ref):
    nq = q_ref.shape[1]
    nk = k_ref.shape[1]
    for rows in _half_tiles(x_ref.shape[0]):
        h = _normmod(x_ref[rows, :], ng_ref[NG_PRE_MIX:NG_PRE_MIX + 1, :],
                     mod_ref[SC1:SC1 + 1, :], mod_ref[SH1:SH1 + 1, :])
        r = _dot(h.astype(BF16), w_ref[...])
        cos = cos_ref[rows, :]
        sin = sin_ref[rows, :]
        for c in range(0, nq, hd):
            qh = _rope(_rms(r[:, c:c + hd]) * gq_ref[...], cos, sin, hd // 4)
            q_ref[rows, c:c + hd] = (qh * scale).astype(BF16)
        for c in range(0, nk, hd):
            k_ref[rows, c:c + hd] = _rope(_rms(r[:, nq + c:nq + c + hd]) * gk_ref[...], cos, sin, hd // 4)
        v_ref[rows, :] = r[:, nq + nk:]


def _qkv_call(geo, x, mod, ng, wqkv, gq, gk, cos, sin):
    m, d = x.shape
    hd = gq.shape[-1]
    assert hd == LANES
    nq, nk = N_HEADS * hd, N_KV_HEADS * hd
    tm = 512
    geo.check_tile(tm)
    kern = functools.partial(_qkv_kernel, hd, hd ** -0.5)
    return pl.pallas_call(
        kern,
        grid=(m // tm,),
        in_specs=[
            pl.BlockSpec((tm, d), lambda i: (i, 0)),
            pl.BlockSpec((None, N_MOD, d), lambda i: (geo.mod_row(i, tm), 0, 0)),
            pl.BlockSpec((4, d), lambda i: (0, 0)),
            pl.BlockSpec((d, nq + 2 * nk), lambda i: (0, 0)),
            pl.BlockSpec((1, hd), lambda i: (0, 0)),
            pl.BlockSpec((1, hd), lambda i: (0, 0)),
            pl.BlockSpec((tm, LANES), lambda i: (geo.rope_blk(i, tm), 0)),
            pl.BlockSpec((tm, LANES), lambda i: (geo.rope_blk(i, tm), 0)),
        ],
        out_specs=[
            pl.BlockSpec((tm, nq), lambda i: (i, 0)),
            pl.BlockSpec((tm, nk), lambda i: (i, 0)),
            pl.BlockSpec((tm, nk), lambda i: (i, 0)),
        ],
        out_shape=[
            jax.ShapeDtypeStruct((m, nq), BF16),
            jax.ShapeDtypeStruct((m, nk), F32),
            jax.ShapeDtypeStruct((m, nk), F32),
        ],
        compiler_params=_cparams(("arbitrary",)),
        name="qkv",
    )(x, mod, ng, wqkv, gq.reshape(1, hd), gk.reshape(1, hd), cos, sin)


def _attn_kernel(n_group, dk, dv, shared_kv, n_kv_in, *refs):
    q_ref, kv_refs, o_ref, scr = refs[0], refs[1:1 + n_kv_in], refs[1 + n_kv_in], list(refs[2 + n_kv_in:])
    kv = []
    for r in kv_refs:
        if r.dtype == BF16:
            kv.append(r)
        else:
            s = scr.pop(0)
            pl.when(pl.program_id(2) == 0)(functools.partial(_store_bf16, s, r))
            kv.append(s)
    has_cache = n_kv_in == 4

    nt = (((1,), (1,)), ((), ()))
    for g in range(n_group):
        kc = slice(0, dk) if shared_kv else slice(g * dk, (g + 1) * dk)
        vc = slice(0, dv) if shared_kv else slice(g * dv, (g + 1) * dv)
        q = q_ref[:, g * dk:(g + 1) * dk]
        s_n = lax.dot_general(q, kv[0][:, kc], nt, preferred_element_type=F32)
        mx = jnp.max(s_n, axis=-1, keepdims=True)
        if has_cache:
            s_c = lax.dot_general(q, kv[2][:, kc], nt, preferred_element_type=F32)
            mx = jnp.maximum(mx, jnp.max(s_c, axis=-1, keepdims=True))
        p_n = jnp.exp(s_n - mx)
        den = jnp.sum(p_n, axis=-1, keepdims=True)
        o = _dot(p_n.astype(BF16), kv[1][:, vc])
        if has_cache:
            p_c = jnp.exp(s_c - mx)
            den = den + jnp.sum(p_c, axis=-1, keepdims=True)
            o = o + _dot(p_c.astype(BF16), kv[3][:, vc])
        o_ref[:, g * dv:(g + 1) * dv] = (o / den).astype(BF16)


def _store_bf16(dst, src):
    dst[...] = src[...].astype(BF16)


def _attn_call(q, kn, vn, kc, vc, *, n_batch, seq, row0, n_kv, n_group, dk, dv, tq, shared_kv, name):
    has_cache = kc is not None
    assert seq % tq == 0 and row0 % seq == 0
    nqt = seq // tq
    qb0, kb0 = row0 // tq, row0 // seq
    kw, vw = (dk, dv) if shared_kv else (n_group * dk, n_group * dv)
    in_specs = [
        pl.BlockSpec((tq, n_group * dk), lambda b, h, t: (qb0 + b * nqt + t, h)),
        pl.BlockSpec((seq, kw), lambda b, h, t: (kb0 + b, h)),
        pl.BlockSpec((seq, vw), lambda b, h, t: (kb0 + b, h)),
    ]
    args = [q, kn, vn]
    rows = [seq, seq]
    if has_cache:
        past = kc.shape[1]
        in_specs += [
            pl.BlockSpec((None, past, kw), lambda b, h, t: (b, 0, h)),
            pl.BlockSpec((None, past, vw), lambda b, h, t: (b, 0, h)),
        ]
        args += [kc, vc]
        rows += [past, past]
    scratch = [pltpu.VMEM((r, w), BF16)
               for a, r, w in zip(args[1:], rows, (kw, vw, kw, vw)) if a.dtype != BF16]
    kern = functools.partial(_attn_kernel, n_group, dk, dv, shared_kv, len(args) - 1)
    return pl.pallas_call(
        kern,
        grid=(n_batch, n_kv, nqt),
        in_specs=in_specs,
        out_specs=pl.BlockSpec((tq, n_group * dv), lambda b, h, t: (b * nqt + t, h)),
        out_shape=jax.ShapeDtypeStruct((n_batch * seq, n_kv * n_group * dv), BF16),
        scratch_shapes=scratch,
        compiler_params=_cparams(("arbitrary", "arbitrary", "arbitrary")),
        name=name,
    )(*args)


def _attn_out_kernel(npt, x_ref, op_ref, os_ref, mod_ref, ng_ref, wo_ref, y_ref):
    def project(o_ref):
        out = _dot(o_ref[...], wo_ref[...])
        y_ref[...] = x_ref[...] + mod_ref[G1:G1 + 1, :] * (
            _rms(out) * ng_ref[NG_POST_MIX:NG_POST_MIX + 1, :])

    i = pl.program_id(0)
    pl.when(i < npt)(lambda: project(op_ref))
    pl.when(i >= npt)(lambda: project(os_ref))


def _attn_out_call(geo, x, o_p, o_s, mod, ng, wo):
    m, d = x.shape
    tm = 512
    geo.check_tile(tm)
    npt = geo.P // tm
    n = o_p.shape[1]
    return pl.pallas_call(
        functools.partial(_attn_out_kernel, npt),
        grid=(m // tm,),
        in_specs=[
            pl.BlockSpec((tm, d), lambda i: (i, 0)),
            pl.BlockSpec((tm, n), lambda i: (jnp.minimum(i, npt - 1), 0)),
            pl.BlockSpec((tm, n), lambda i: (jnp.maximum(i - npt, 0), 0)),
            pl.BlockSpec((None, N_MOD, d), lambda i: (geo.mod_row(i, tm), 0, 0)),
            pl.BlockSpec((4, d), lambda i: (0, 0)),
            pl.BlockSpec(wo.shape, lambda i: (0, 0)),
        ],
        out_specs=pl.BlockSpec((tm, d), lambda i: (i, 0)),
        out_shape=jax.ShapeDtypeStruct((m, d), F32),
        compiler_params=_cparams(("arbitrary",)),
        name="attn_out",
    )(x, o_p, o_s, mod, ng, wo)


HI_MASK = 0xFFFF0000


def _as_bf16_bits(v):
    return lax.bitcast_convert_type(v.astype(BF16).astype(F32), jnp.uint32)


def _pack_rows(v, out_ref):
    n, d = v.shape
    assert d == 2 * SUBLANES * LANES
    for s in range(SUBLANES):
        lo = _as_bf16_bits(v[:, s * LANES:(s + 1) * LANES])
        hi = _as_bf16_bits(v[:, (s + SUBLANES) * LANES:(s + SUBLANES + 1) * LANES])
        out_ref[pl.ds(s, n, stride=SUBLANES), :] = (lo >> 16) | (hi & jnp.uint32(HI_MASK))


def _unpack_rows(buf, n, s):
    w = buf[pl.ds(s, n, stride=SUBLANES), :]
    lo = lax.bitcast_convert_type(w << 16, F32)
    hi = lax.bitcast_convert_type(w & jnp.uint32(HI_MASK), F32)
    return lo, hi


def _tile_copy(src_hbm, src_row, dst, dst_row, sem):
    return pltpu.make_async_copy(
        src_hbm.at[pl.ds(pl.multiple_of(src_row * SUBLANES, SUBLANES), SUBLANES), :],
        dst.at[pl.ds(pl.multiple_of(dst_row * SUBLANES, SUBLANES), SUBLANES), :], sem)


def _start_gather(idx_ref, src_hbm, dst, n, sem):
    def start(r, c):
        _tile_copy(src_hbm, idx_ref[0, r], dst, r, sem).start()
        return c

    lax.fori_loop(0, n, start, 0, unroll=8)


def _wait_gather(src_hbm, dst, n, sem):
    pltpu.make_async_copy(src_hbm.at[pl.ds(0, n * SUBLANES), :], dst, sem).wait()


def _pipelined_gather(i, n_steps, idx_ref, nxt_ref, src_hbm, bufs, n, sems, consume):
    for sl in (0, 1):
        @pl.when(i % 2 == sl)
        def _(sl=sl):
            if sl == 0:
                @pl.when(i == 0)
                def _():
                    _start_gather(idx_ref, src_hbm, bufs[0], n, sems.at[0])

            @pl.when(i + 1 < n_steps)
            def _():
                _start_gather(nxt_ref, src_hbm, bufs[1 - sl], n, sems.at[1 - sl])

            _wait_gather(src_hbm, bufs[sl], n, sems.at[sl])
            consume(bufs[sl])


def _route_kernel(n_exp, x_ref, mod_ref, ng_ref, wr_ref, br_ref, h_ref, topw_ref, topi_ref):
    h = _normmod(x_ref[...], ng_ref[NG_PRE_FFN:NG_PRE_FFN + 1, :],
                 mod_ref[SC2:SC2 + 1, :], mod_ref[SH2:SH2 + 1, :])
    _pack_rows(h, h_ref)
    logits = jnp.dot(h, wr_ref[...], preferred_element_type=F32,
                     precision=lax.Precision.HIGHEST) + br_ref[...]
    lane = lax.broadcasted_iota(jnp.int32, logits.shape, 1)
    neg = jnp.float32(-jnp.inf)
    logits = jnp.where(lane < n_exp, logits, neg)
    m1 = jnp.max(logits, axis=-1, keepdims=True)
    i1 = jnp.min(jnp.where(logits == m1, lane, LANES), axis=-1, keepdims=True)
    rest = jnp.where(lane == i1, neg, logits)
    m2 = jnp.max(rest, axis=-1, keepdims=True)
    i2 = jnp.min(jnp.where(rest == m2, lane, LANES), axis=-1, keepdims=True)
    e = jnp.exp(m2 - m1)
    w1 = 1.0 / (1.0 + e)
    w2 = e / (1.0 + e)
    topw_ref[...] = jnp.where(lane == 0, w1, jnp.where(lane == 1, w2, 0.0))
    topi_ref[...] = jnp.where(lane == 0, i1, jnp.where(lane == 1, i2, 0))


def _route_call(geo, x, mod, ng, wr_pad, br_pad):
    m, d = x.shape
    tm = 512
    geo.check_tile(tm)
    kern = functools.partial(_route_kernel, N_EXPERTS)
    return pl.pallas_call(
        kern,
        grid=(m // tm,),
        in_specs=[
            pl.BlockSpec((tm, d), lambda i: (i, 0)),
            pl.BlockSpec((None, N_MOD, d), lambda i: (geo.mod_row(i, tm), 0, 0)),
            pl.BlockSpec((4, d), lambda i: (0, 0)),
            pl.BlockSpec((d, LANES), lambda i: (0, 0)),
            pl.BlockSpec((1, LANES), lambda i: (0, 0)),
        ],
        out_specs=[
            pl.BlockSpec((tm * SUBLANES, LANES), lambda i: (i, 0)),
            pl.BlockSpec((tm, LANES), lambda i: (i, 0)),
            pl.BlockSpec((tm, LANES), lambda i: (i, 0)),
        ],
        out_shape=[
            jax.ShapeDtypeStruct((m * SUBLANES, LANES), jnp.uint32),
            jax.ShapeDtypeStruct((m, LANES), F32),
            jax.ShapeDtypeStruct((m, LANES), jnp.int32),
        ],
        compiler_params=_cparams(("arbitrary",)),
        name="route",
    )(x, mod, ng, wr_pad, br_pad)


def _dispatch(topi, topw, tr):
    m = topi.shape[0]
    n_asg = TOP_K * m
    n_tiles = n_asg // tr + N_EXPERTS
    e = topi[:, :TOP_K].reshape(n_asg)
    w = topw[:, :TOP_K].reshape(n_asg)
    onehot = (e[:, None] == jnp.arange(N_EXPERTS, dtype=jnp.int32)[None, :]).astype(jnp.int32)
    csum = jnp.cumsum(onehot, axis=0)
    count = csum[-1]
    rank = jnp.sum(csum * onehot, axis=1) - 1
    ptiles = (count + tr - 1) // tr
    tile_end = jnp.cumsum(ptiles)
    tile_start = tile_end - ptiles
    n_used = tile_end[-1:]
    dest = tile_start[e] * tr + rank
    asg = jnp.full((n_tiles * tr,), -1, jnp.int32).at[dest].set(jnp.arange(n_asg, dtype=jnp.int32))
    src = jnp.maximum(asg, 0) // TOP_K
    gate = jnp.where(asg >= 0, w[jnp.maximum(asg, 0)], 0.0)
    tile_expert = jnp.minimum(
        jnp.sum((jnp.arange(n_tiles, dtype=jnp.int32)[:, None] >= tile_end[None, :]).astype(jnp.int32), axis=1),
        N_EXPERTS - 1)
    return src, gate, tile_expert, n_used.astype(jnp.int32), dest


def _moe_kernel(te_ref, nu_ref, src_ref, nxt_ref, h_hbm, g_ref, wg_ref, wu_ref, wd_ref, y_ref,
                buf0, buf1, hbf, acc, sems):
    i = pl.program_id(0)
    f = pl.program_id(1)
    n_used = nu_ref[0]
    used = i < n_used
    tr = hbf.shape[0]

    def unpack(buf):
        for s in range(SUBLANES):
            lo, hi = _unpack_rows(buf, tr, s)
            hbf[:, s * LANES:(s + 1) * LANES] = lo.astype(BF16)
            hbf[:, (s + SUBLANES) * LANES:(s + SUBLANES + 1) * LANES] = hi.astype(BF16)

    @pl.when(used & (f == 0))
    def _():
        _pipelined_gather(i, n_used, src_ref, nxt_ref, h_hbm, (buf0, buf1), tr, sems, unpack)
        acc[...] = jnp.zeros_like(acc)

    @pl.when(used)
    def _():
        h = hbf[...]
        g = g_ref[...]
        tf = wg_ref.shape[1]
        for c0 in range(0, tf, MOE_TC):
            c1 = min(c0 + MOE_TC, tf)
            t = _silu(_dot(h, wg_ref[:, c0:c1])) * _dot(h, wu_ref[:, c0:c1]) * g
            acc[...] += _dot(t.astype(BF16), wd_ref[c0:c1, :])

    last = f == pl.num_programs(1) - 1

    @pl.when(used & last)
    def _():
        _pack_rows(acc[...], y_ref)

    @pl.when(jnp.logical_not(used) & last)
    def _():
        y_ref[...] = jnp.zeros_like(y_ref)


def _moe_call(h, src, gate, tile_expert, n_used, wg, wu, wd, lyr, tr):
    _, n_exp, d, dfe = wg.shape
    n_tiles = tile_expert.shape[0]
    tf = MOE_TF
    assert dfe % tf == 0
    nf = dfe // tf

    def fblk(i, f, nu):
        return jnp.where(i < nu[0], f, nf - 1)

    src = src.reshape(n_tiles, 1, tr)
    tile_rows = tr * SUBLANES
    grid_spec = pltpu.PrefetchScalarGridSpec(
        num_scalar_prefetch=2,
        grid=(n_tiles, nf),
        in_specs=[
            pl.BlockSpec((None, 1, tr), lambda i, f, te, nu: (i, 0, 0), memory_space=pltpu.SMEM),
            pl.BlockSpec((None, 1, tr), lambda i, f, te, nu: (jnp.minimum(i + 1, n_tiles - 1), 0, 0),
                         memory_space=pltpu.SMEM),
            pl.BlockSpec(memory_space=pl.ANY),
            pl.BlockSpec((tr, 1), lambda i, f, te, nu: (i, 0)),
            pl.BlockSpec((None, None, d, tf), lambda i, f, te, nu: (lyr, te[i], 0, fblk(i, f, nu))),
            pl.BlockSpec((None, None, d, tf), lambda i, f, te, nu: (lyr, te[i], 0, fblk(i, f, nu))),
            pl.BlockSpec((None, None, tf, d), lambda i, f, te, nu: (lyr, te[i], fblk(i, f, nu), 0)),
        ],
        out_specs=pl.BlockSpec((tile_rows, LANES), lambda i, f, te, nu: (i, 0)),
        scratch_shapes=[pltpu.VMEM((tile_rows, LANES), jnp.uint32), pltpu.VMEM((tile_rows, LANES), jnp.uint32),
                        pltpu.VMEM((tr, d), BF16), pltpu.VMEM((tr, d), F32),
                        pltpu.SemaphoreType.DMA((2,))],
    )
    return pl.pallas_call(
        _moe_kernel,
        grid_spec=grid_spec,
        out_shape=jax.ShapeDtypeStruct((n_tiles * tile_rows, LANES), jnp.uint32),
        compiler_params=_cparams(("arbitrary", "arbitrary")),
        name="moe",
    )(tile_expert, n_used, src, src, h, gate.reshape(n_tiles * tr, 1), wg, wu, wd)


def _combine_kernel(npt, pos_ref, nxt_ref, y_hbm, x_ref, mod_ref, ng_ref, *refs):
    outs, (buf0, buf1, ysum, sems) = refs[:-4], refs[-4:]
    i = pl.program_id(0)
    tm = x_ref.shape[0]

    def add_pairs(buf):
        for s in range(SUBLANES):
            lo, hi = _unpack_rows(buf, TOP_K * tm, s)
            ysum[:, s * LANES:(s + 1) * LANES] = lo[:tm] + lo[tm:]
            ysum[:, (s + SUBLANES) * LANES:(s + SUBLANES + 1) * LANES] = hi[:tm] + hi[tm:]

    _pipelined_gather(i, pl.num_programs(0), pos_ref, nxt_ref, y_hbm, (buf0, buf1), TOP_K * tm, sems,
                      add_pairs)
    res = x_ref[...] + mod_ref[G2:G2 + 1, :] * (
        _rms(ysum[...]) * ng_ref[NG_POST_FFN:NG_POST_FFN + 1, :])
    if npt is None:
        outs[0][...] = res
    else:
        @pl.when(i < npt)
        def _():
            outs[0][...] = res

        @pl.when(i >= npt)
        def _():
            outs[1][...] = res


def _combine_call(geo, x, y, dest, mod, ng, split):
    m, d = x.shape
    tm = 256
    geo.check_tile(tm)
    nt = m // tm
    npt = geo.P // tm
    pos = dest.reshape(nt, tm, TOP_K).transpose(0, 2, 1).reshape(nt, 1, TOP_K * tm)
    if split:
        out_specs = [pl.BlockSpec((tm, d), lambda i: (jnp.minimum(i, npt - 1), 0)),
                     pl.BlockSpec((tm, d), lambda i: (jnp.maximum(i - npt, 0), 0))]
        out_shape = [jax.ShapeDtypeStruct((geo.P, d), F32), jax.ShapeDtypeStruct((m - geo.P, d), F32)]
    else:
        out_specs = pl.BlockSpec((tm, d), lambda i: (i, 0))
        out_shape = jax.ShapeDtypeStruct((m, d), F32)
    buf = pltpu.VMEM((TOP_K * tm * SUBLANES, LANES), jnp.uint32)
    return pl.pallas_call(
        functools.partial(_combine_kernel, npt if split else None),
        grid=(nt,),
        in_specs=[
            pl.BlockSpec((None, 1, TOP_K * tm), lambda i: (i, 0, 0), memory_space=pltpu.SMEM),
            pl.BlockSpec((None, 1, TOP_K * tm), lambda i: (jnp.minimum(i + 1, nt - 1), 0, 0),
                         memory_space=pltpu.SMEM),
            pl.BlockSpec(memory_space=pl.ANY),
            pl.BlockSpec((tm, d), lambda i: (i, 0)),
            pl.BlockSpec((None, N_MOD, d), lambda i: (geo.mod_row(i, tm), 0, 0)),
            pl.BlockSpec((4, d), lambda i: (0, 0)),
        ],
        out_specs=out_specs,
        out_shape=out_shape,
        scratch_shapes=[buf, buf, pltpu.VMEM((tm, d), F32), pltpu.SemaphoreType.DMA((2,))],
        compiler_params=_cparams(("arbitrary",)),
        name="moe_combine",
    )(pos, pos, y, x, mod, ng)


def _mla_in_kernel(scale, x_ref, mod_ref, ng_ref, wd_ref, gq_ref, gkv_ref, wuq_ref, wuk_ref, wuv_ref,
                   cos_ref, sin_ref, q_ref, kcat_ref, v_ref, ckv_ref, kpe_ref):
    qr = gq_ref.shape[1]
    kr = gkv_ref.shape[1]
    rope_dim = kpe_ref.shape[1]
    h = _normmod(x_ref[...], ng_ref[NG_PRE_MIX:NG_PRE_MIX + 1, :],
                 mod_ref[SC1:SC1 + 1, :], mod_ref[SH1:SH1 + 1, :])
    r = _dot(h.astype(BF16), wd_ref[...])
    cos = cos_ref[...]
    sin = sin_ref[...]
    cq = (_rms(r[:, :qr]) * gq_ref[...]).astype(BF16)
    ckv = _rms(r[:, qr:qr + kr]) * gkv_ref[...]
    kpe = r[:, qr + kr:qr + kr + LANES]
    ckv_ref[...] = ckv
    kpe_ref[...] = kpe[:, :rope_dim]
    ckv_b = ckv.astype(BF16)
    q = _dot(cq, wuq_ref[...])
    kn = _dot(ckv_b, wuk_ref[...])
    v_ref[...] = _dot(ckv_b, wuv_ref[...]).astype(BF16)
    kpe_r = _rope(kpe, cos, sin, rope_dim // 4).astype(BF16)
    for hh in range(q_ref.shape[1] // MLA_QK):
        c = hh * MLA_QK
        q_ref[:, c:c + NOPE_DIM] = (q[:, c:c + NOPE_DIM] * scale).astype(BF16)
        q_ref[:, c + NOPE_DIM:c + MLA_QK] = (
            _rope(q[:, c + NOPE_DIM:c + MLA_QK], cos, sin, rope_dim // 4) * scale).astype(BF16)
        kcat_ref[:, c:c + NOPE_DIM] = kn[:, hh * NOPE_DIM:(hh + 1) * NOPE_DIM].astype(BF16)
        kcat_ref[:, c + NOPE_DIM:c + MLA_QK] = kpe_r


def _mla_in_call(geo, x, mod, ng, wd_cat, gq, gkv, wuq_pad, wuk, wuv, cos, sin):
    m, d = x.shape
    qr, kr = gq.shape[-1], gkv.shape[-1]
    nqk = wuq_pad.shape[1]
    nv = wuv.shape[1]
    tm = 512
    geo.check_tile(tm)
    kern = functools.partial(_mla_in_kernel, (NOPE_DIM + ROPE_DIM) ** -0.5)
    full = lambda a: pl.BlockSpec(a.shape, lambda i: (0, 0))
    row = lambda n: pl.BlockSpec((tm, n), lambda i: (i, 0))
    return pl.pallas_call(
        kern,
        grid=(m // tm,),
        in_specs=[
            row(d),
            pl.BlockSpec((None, N_MOD, d), lambda i: (geo.mod_row(i, tm), 0, 0)),
            pl.BlockSpec((4, d), lambda i: (0, 0)),
            full(wd_cat),
            pl.BlockSpec((1, qr), lambda i: (0, 0)),
            pl.BlockSpec((1, kr), lambda i: (0, 0)),
            full(wuq_pad), full(wuk), full(wuv),
            pl.BlockSpec((tm, LANES), lambda i: (geo.rope_blk(i, tm), 0)),
            pl.BlockSpec((tm, LANES), lambda i: (geo.rope_blk(i, tm), 0)),
        ],
        out_specs=[row(nqk), row(nqk), row(nv), row(kr), row(ROPE_DIM)],
        out_shape=[
            jax.ShapeDtypeStruct((m, nqk), BF16),
            jax.ShapeDtypeStruct((m, nqk), BF16),
            jax.ShapeDtypeStruct((m, nv), BF16),
            jax.ShapeDtypeStruct((m, kr), F32),
            jax.ShapeDtypeStruct((m, ROPE_DIM), F32),
        ],
        compiler_params=_cparams(("arbitrary",)),
        name="mla_in",
    )(x, mod, ng, wd_cat, gq.reshape(1, qr), gkv.reshape(1, kr), wuq_pad, wuk, wuv, cos, sin)


def _mla_cache_kernel(ckv_ref, kpe_ref, wuk_ref, wuv_ref, kcat_ref, v_ref):
    ckv = ckv_ref[...].astype(BF16)
    kn = _dot(ckv, wuk_ref[...])
    v_ref[...] = _dot(ckv, wuv_ref[...]).astype(BF16)
    kpe = kpe_ref[...].astype(BF16)
    for hh in range(kcat_ref.shape[1] // MLA_QK):
        c = hh * MLA_QK
        kcat_ref[:, c:c + NOPE_DIM] = kn[:, hh * NOPE_DIM:(hh + 1) * NOPE_DIM].astype(BF16)
        kcat_ref[:, c + NOPE_DIM:c + MLA_QK] = kpe


def _mla_cache_call(ckv, kpe_pad, wuk, wuv):
    n, kr = ckv.shape
    tm = 512
    assert n % tm == 0
    nqk = MLA_HEADS * MLA_QK
    nv = wuv.shape[1]
    return pl.pallas_call(
        _mla_cache_kernel,
        grid=(n // tm,),
        in_specs=[
            pl.BlockSpec((tm, kr), lambda i: (i, 0)),
            pl.BlockSpec((tm, LANES), lambda i: (i, 0)),
            pl.BlockSpec(wuk.shape, lambda i: (0, 0)),
            pl.BlockSpec(wuv.shape, lambda i: (0, 0)),
        ],
        out_specs=[pl.BlockSpec((tm, nqk), lambda i: (i, 0)), pl.BlockSpec((tm, nv), lambda i: (i, 0))],
        out_shape=[jax.ShapeDtypeStruct((n, nqk), BF16), jax.ShapeDtypeStruct((n, nv), BF16)],
        compiler_params=_cparams(("arbitrary",)),
        name="mla_cache",
    )(ckv, kpe_pad, wuk, wuv)


def _rope_table(n_tokens, dim, n_ident):
    rows = n_tokens // GRID_W
    row = jnp.repeat(jnp.arange(rows, dtype=F32), GRID_W)
    col = jnp.tile(jnp.arange(GRID_W, dtype=F32), rows)
    half = dim // 2
    inv = ROPE_THETA ** (-jnp.arange(0, half, 2, dtype=F32) / half)
    ar = row[:, None] * inv[None, :]
    ac = col[:, None] * inv[None, :]
    ang = jnp.concatenate([ar, ar, ac, ac], axis=-1)
    cos = jnp.cos(ang)
    sin = jnp.sin(ang)
    hb = dim // 4
    first = (jnp.arange(dim) % (2 * hb)) < hb
    sin = jnp.where(first[None, :], -sin, sin)
    cos = jnp.pad(cos, ((0, n_ident), (0, LANES - dim)), constant_values=1.0)
    sin = jnp.pad(sin, ((0, n_ident), (0, LANES - dim)))
    return cos, sin


def kernel(x_prompt, x_sample, cache_gqa_k, cache_gqa_v, cache_mla_ckv, cache_mla_kpe, c, c_ctx,
           w_mod, b_mod, norm_g,
           conv_w1, conv_b1, conv_wdw, conv_bdw, conv_ln_g, conv_ln_b, conv_w2, conv_b2,
           gqa_wq, gqa_wk, gqa_wv, gqa_gq, gqa_gk, gqa_wo,
           mla_wdq, mla_gq, mla_wuq, mla_wdkv, mla_gkv, mla_wuk, mla_wuv, mla_wo,
           ffn_wg, ffn_wu, ffn_wd,
           moe_wr, moe_br, moe_wg, moe_wu, moe_wd):
    b, s, d = x_prompt.shape
    db, t, _ = x_sample.shape
    geo = _Geom(b, s, db, t)
    depth = w_mod.shape[0]
    past = cache_gqa_k.shape[2]
    hd = gqa_gq.shape[-1]
    bf = lambda a: a.astype(BF16)

    assert N_MIXERS == 3
    x = (x_prompt.reshape(geo.P, d), x_sample.reshape(db * t, d))

    n_cond = 1 + db
    cond = jnp.concatenate([c_ctx[None, :], c], axis=0)
    cond = jnp.pad(cond, ((0, -n_cond % 8), (0, 0)))
    mods = _mod_call(cond, w_mod, b_mod).reshape(depth, cond.shape[0], N_MOD, d)

    rope_tm = 512
    cos_hd, sin_hd = _rope_table(t, hd, rope_tm)
    cos_pe, sin_pe = _rope_table(t, ROPE_DIM, rope_tm)

    conv_w1_b, conv_w2_b = bf(conv_w1), bf(conv_w2)
    ffn_wg_b, ffn_wu_b, ffn_wd_b = bf(ffn_wg), bf(ffn_wu), bf(ffn_wd)
    moe_wg_b, moe_wu_b, moe_wd_b = bf(moe_wg), bf(moe_wu), bf(moe_wd)

    ks, vs, cs, ps = [], [], [], []
    for i in range(depth):
        mod, ng = mods[i], norm_g[i]
        j = i // N_MIXERS
        if i % N_MIXERS == 0:
            u = _conv_in_call(geo, x, mod, ng, conv_w1_b, j, conv_b1[j])
            x = _conv_out_call(geo, u, x, mod, ng, conv_wdw[j], conv_bdw[j], conv_ln_g[j],
                               conv_ln_b[j], conv_w2_b, j, conv_b2[j])
        elif i % N_MIXERS == 1:
            wqkv = bf(jnp.concatenate([gqa_wq[j], gqa_wk[j], gqa_wv[j]], axis=1))
            q, k, v = _qkv_call(geo, x, mod, ng, wqkv, gqa_gq[j], gqa_gk[j], cos_hd, sin_hd)
            ks.append(k[:geo.P].reshape(b, s, N_KV_HEADS, hd))
            vs.append(v[:geo.P].reshape(b, s, N_KV_HEADS, hd))
            kw = dict(n_kv=N_KV_HEADS, n_group=N_HEADS // N_KV_HEADS, dk=hd, dv=hd, shared_kv=True)
            o_p = _attn_call(q, k, v, None, None, n_batch=b, seq=s, row0=0, tq=min(s, 256),
                             name="gqa_attn_prompt", **kw)
            o_s = _attn_call(q, k, v,
                             cache_gqa_k[:, j].reshape(db, past, N_KV_HEADS * hd),
                             cache_gqa_v[:, j].reshape(db, past, N_KV_HEADS * hd),
                             n_batch=db, seq=t, row0=geo.P, tq=256, name="gqa_attn_sample", **kw)
            x = _attn_out_call(geo, x, o_p, o_s, mod, ng, bf(gqa_wo[j]))
        else:
            qr, kr = mla_gq.shape[-1], mla_gkv.shape[-1]
            wd_cat = bf(jnp.pad(jnp.concatenate([mla_wdq[j], mla_wdkv[j]], axis=1),
                                ((0, 0), (0, LANES - ROPE_DIM))))
            wuq_pad = bf(jnp.pad(mla_wuq[j], ((0, 0), (0, 0), (0, MLA_QK - NOPE_DIM - ROPE_DIM)))
                         ).reshape(qr, MLA_HEADS * MLA_QK)
            wuk = bf(mla_wuk[j]).reshape(kr, MLA_HEADS * NOPE_DIM)
            wuv = bf(mla_wuv[j]).reshape(kr, MLA_HEADS * V_DIM)
            q, kcat, v, ckv, kpe = _mla_in_call(geo, x, mod, ng, wd_cat, mla_gq[j], mla_gkv[j],
                                                wuq_pad, wuk, wuv, cos_pe, sin_pe)
            cs.append(ckv[:geo.P].reshape(b, s, kr))
            ps.append(kpe[:geo.P].reshape(b, s, ROPE_DIM))
            kc, vc = _mla_cache_call(
                cache_mla_ckv[:, j].reshape(db * past, kr),
                jnp.pad(cache_mla_kpe[:, j].reshape(db * past, ROPE_DIM), ((0, 0), (0, LANES - ROPE_DIM))),
                wuk, wuv)
            kw = dict(n_kv=MLA_HEADS // MLA_HPS, n_group=MLA_HPS, dk=MLA_QK, dv=V_DIM, shared_kv=False)
            o_p = _attn_call(q, kcat, v, None, None, n_batch=b, seq=s, row0=0, tq=min(s, 256),
                             name="mla_attn_prompt", **kw)
            o_s = _attn_call(q, kcat, v, kc.reshape(db, past, -1), vc.reshape(db, past, -1),
                             n_batch=db, seq=t, row0=geo.P, tq=256, name="mla_attn_sample", **kw)
            x = _attn_out_call(geo, x, o_p, o_s, mod, ng, bf(mla_wo[j]))
        f = i // 2
        if i % 2 == 0:
            x = _ffn_call(geo, x, mod, ng, ffn_wg_b, ffn_wu_b, ffn_wd_b, f)
        else:
            wr_pad = jnp.pad(moe_wr[f], ((0, 0), (0, LANES - N_EXPERTS)))
            br_pad = jnp.pad(moe_br[f], (0, LANES - N_EXPERTS)).reshape(1, LANES)
            h, topw, topi = _route_call(geo, x, mod, ng, wr_pad, br_pad)
            src, gate, tile_expert, n_used, dest = _dispatch(topi, topw, MOE_TR)
            y = _moe_call(h, src, gate, tile_expert, n_used, moe_wg_b, moe_wu_b, moe_wd_b, f, MOE_TR)
            x = _combine_call(geo, x, y, dest, mod, ng, split=(i == depth - 1))

    y_prompt, y_sample = x if isinstance(x, (list, tuple)) else (x[:geo.P], x[geo.P:])
    return (y_prompt.reshape(b, s, d), y_sample.reshape(db, t, d), jnp.stack(ks, axis=1), jnp.stack(vs, axis=1),
            jnp.stack(cs, axis=1), jnp.stack(ps, axis=1))
```

```python
import functools

import jax
import jax.numpy as jnp
from jax import lax
from jax.experimental import pallas as pl
from jax.experimental.pallas import tpu as pltpu

F32 = jnp.float32
BF16 = jnp.bfloat16

EPS = 1e-6
GRID_W = 64
ROPE_THETA = 10000.0
N_MIXERS = 3
N_MOD = 6
CONV_W = 31
CONV_PAD = CONV_W // 2
N_HEADS = 16
N_KV_HEADS = 4
MLA_HEADS = 16
NOPE_DIM = 128
ROPE_DIM = 64
V_DIM = 128
N_EXPERTS = 8
TOP_K = 2
MOE_TR = 512
MOE_TF = 1408
MOE_TC = 256

LANES = 128
SUBLANES = 8
HALO = 16
MLA_QK = 256
MLA_HPS = 4
VMEM_LIMIT = 56 * 1024 * 1024

SH1, SC1, G1, SH2, SC2, G2 = range(6)
NG_PRE_MIX, NG_POST_MIX, NG_PRE_FFN, NG_POST_FFN = range(4)


def _cparams(sem):
    return pltpu.CompilerParams(dimension_semantics=sem, vmem_limit_bytes=VMEM_LIMIT)


def _rms(x):
    return x * lax.rsqrt(jnp.mean(x * x, axis=-1, keepdims=True) + EPS)


def _normmod(x, g, sc, sh):
    return (_rms(x) * g) * (1.0 + sc) + sh


def _silu(x):
    return x * jax.nn.sigmoid(x)


def _dot(a, b):
    return jnp.dot(a, b, preferred_element_type=F32)


def _half_tiles(tm):
    half = tm // 2
    return (slice(0, half), slice(half, tm))


def _rope(x, cos, sin_signed, hb):
    lane = lax.broadcasted_iota(jnp.int32, x.shape, 1)
    first = (lane % (2 * hb)) < hb
    rot = jnp.where(first, pltpu.roll(x, LANES - hb, 1), pltpu.roll(x, hb, 1))
    return x * cos + rot * sin_signed


class _Geom:
    def __init__(self, b, s, db, t):
        self.B, self.S, self.DB, self.T = b, s, db, t
        self.P = b * s
        self.M = self.P + db * t

    def check_tile(self, tm):
        assert self.P % tm == 0 and self.T % tm == 0, (self.P, self.T, tm)

    def mod_row(self, i, tm):
        npt = self.P // tm
        return jnp.where(i < npt, 0, 1 + (i - npt) // (self.T // tm))

    def rope_blk(self, i, tm):
        npt = self.P // tm
        return jnp.where(i < npt, self.T // tm, (i - npt) % (self.T // tm))


def _mod_kernel(c_ref, w_ref, b_ref, o_ref):
    c = c_ref[...]
    o_ref[...] = _dot(_silu(c).astype(BF16), w_ref[...].astype(BF16)) + b_ref[...]


def _mod_call(cond, w_mod, b_mod):
    depth, d, n = w_mod.shape
    rows = cond.shape[0]
    tn = 1024
    return pl.pallas_call(
        _mod_kernel,
        grid=(depth, n // tn),
        in_specs=[
            pl.BlockSpec((rows, d), lambda l, j: (0, 0)),
            pl.BlockSpec((None, d, tn), lambda l, j: (l, 0, j)),
            pl.BlockSpec((None, 1, tn), lambda l, j: (l, 0, j)),
        ],
        out_specs=pl.BlockSpec((None, rows, tn), lambda l, j: (l, 0, j)),
        out_shape=jax.ShapeDtypeStruct((depth, rows, n), F32),
        compiler_params=_cparams(("arbitrary", "arbitrary")),
        name="mod",
    )(cond, w_mod, b_mod.reshape(depth, 1, n))


def _x_specs(geo, x, tm, n_grid):
    def spec(f):
        return pl.BlockSpec((tm, x[0].shape[1] if isinstance(x, tuple) else x.shape[1]),
                            (lambda i: (f(i), 0)) if n_grid == 1 else (lambda i, j: (f(i), 0)))

    if not isinstance(x, tuple):
        return (x,), [spec(lambda i: i)]
    npt = geo.P // tm
    return x, [spec(lambda i: jnp.minimum(i, npt - 1)), spec(lambda i: jnp.maximum(i - npt, 0))]


def _with_x_tile(i, npt, x_refs, fn):
    if len(x_refs) == 1:
        fn(x_refs[0])
    else:
        pl.when(i < npt)(lambda: fn(x_refs[0]))
        pl.when(i >= npt)(lambda: fn(x_refs[1]))


def _conv_in_kernel(n_x, npt, *refs):
    x_refs = refs[:n_x]
    mod_ref, ng_ref, wa_ref, wg_ref, ba_ref, bg_ref, u_ref, h_scr = refs[n_x:]

    def prologue(x_ref):
        h = _normmod(x_ref[...], ng_ref[NG_PRE_MIX:NG_PRE_MIX + 1, :],
                     mod_ref[SC1:SC1 + 1, :], mod_ref[SH1:SH1 + 1, :])
        h_scr[...] = h.astype(BF16)

    @pl.when(pl.program_id(1) == 0)
    def _():
        _with_x_tile(pl.program_id(0), npt, x_refs, prologue)

    h = h_scr[...]
    a = _dot(h, wa_ref[...]) + ba_ref[...]
    g = _dot(h, wg_ref[...]) + bg_ref[...]
    u_ref[...] = a * jax.nn.sigmoid(g)


def _conv_in_call(geo, x, mod, ng, w1, lyr, b1):
    m, d = geo.M, w1.shape[1]
    tm, tn = 512, 1024
    geo.check_tile(tm)
    nj = d // tn
    b1 = b1.reshape(1, 2 * d)
    xs, x_specs = _x_specs(geo, x, tm, 2)
    return pl.pallas_call(
        functools.partial(_conv_in_kernel, len(xs), geo.P // tm),
        grid=(m // tm, nj),
        in_specs=x_specs + [
            pl.BlockSpec((None, N_MOD, d), lambda i, j: (geo.mod_row(i, tm), 0, 0)),
            pl.BlockSpec((4, d), lambda i, j: (0, 0)),
            pl.BlockSpec((None, d, tn), lambda i, j: (lyr, 0, j)),
            pl.BlockSpec((None, d, tn), lambda i, j: (lyr, 0, j + nj)),
            pl.BlockSpec((1, tn), lambda i, j: (0, j)),
            pl.BlockSpec((1, tn), lambda i, j: (0, j + nj)),
        ],
        out_specs=pl.BlockSpec((tm, tn), lambda i, j: (i, j)),
        out_shape=jax.ShapeDtypeStruct((m, d), F32),
        scratch_shapes=[pltpu.VMEM((tm, d), BF16)],
        compiler_params=_cparams(("arbitrary", "arbitrary")),
        name="conv_in",
    )(*xs, mod, ng, w1, w1, b1, b1)


CONV_TM = 256
CONV_RC = 32
CONV_CC = 256


def _conv_out_kernel(n_x, tiles_p, tps_p, tps_s, *refs):
    x_refs = refs[:n_x]
    (u_ref, up_ref, un_ref, mod_ref, ng_ref, wdw_ref, bdw_ref,
     lng_ref, lnb_ref, w2_ref, b2_ref, o_ref, ext, cv, sh) = refs[n_x:]
    i = pl.program_id(0)
    is_p = i < tiles_p
    j = jnp.where(is_p, i % tps_p, (i - tiles_p) % tps_s)
    n = jnp.where(is_p, tps_p, tps_s)
    tm, d = u_ref.shape
    ext[HALO:HALO + tm, :] = u_ref[...]
    ext[0:HALO, :] = jnp.where(j == 0, 0.0, up_ref[...])
    ext[HALO + tm:HALO + tm + HALO, :] = jnp.where(j == n - 1, 0.0, un_ref[...])

    def chunk(c, carry):
        c0 = pl.multiple_of(c * CONV_CC, CONV_CC)
        e = ext[:, pl.ds(c0, CONV_CC)]
        for p in range(1, SUBLANES):
            sh[p - 1] = pltpu.roll(e, e.shape[0] - p, 0)
        for r in range(0, tm, CONV_RC):
            acc = jnp.zeros((CONV_RC, CONV_CC), F32)
            for k in range(CONV_W):
                r0 = HALO - CONV_PAD + r + k
                p = r0 % SUBLANES
                a0 = r0 - p
                if p == 0:
                    tap = ext[a0:a0 + CONV_RC, pl.ds(c0, CONV_CC)]
                else:
                    tap = sh[p - 1, a0:a0 + CONV_RC, :]
                acc = acc + tap * wdw_ref[k:k + 1, pl.ds(c0, CONV_CC)]
            cv[r:r + CONV_RC, pl.ds(c0, CONV_CC)] = acc
        return carry

    lax.fori_loop(0, d // CONV_CC, chunk, 0)

    v = cv[...] + bdw_ref[...]
    mu = jnp.mean(v, axis=-1, keepdims=True)
    xc = v - mu
    var = jnp.mean(xc * xc, axis=-1, keepdims=True)
    y = _silu(xc * lax.rsqrt(var + EPS) * lng_ref[...] + lnb_ref[...])
    out = _dot(y.astype(BF16), w2_ref[...]) + b2_ref[...]
    upd = mod_ref[G1:G1 + 1, :] * (_rms(out) * ng_ref[NG_POST_MIX:NG_POST_MIX + 1, :])

    def residual(x_ref):
        o_ref[...] = x_ref[...] + upd

    _with_x_tile(i, tiles_p, x_refs, residual)


def _conv_out_call(geo, u, x, mod, ng, wdw, bdw, lng, lnb, w2, lyr, b2):
    m, d = u.shape
    tm = CONV_TM
    assert geo.S % tm == 0 and geo.T % tm == 0
    hb = tm // HALO
    nhb = m // HALO
    xs, x_specs = _x_specs(geo, x, tm, 1)
    kern = functools.partial(_conv_out_kernel, len(xs), geo.P // tm, geo.S // tm, geo.T // tm)
    vec = lambda a: a.reshape(1, d)
    return pl.pallas_call(
        kern,
        grid=(m // tm,),
        in_specs=x_specs + [
            pl.BlockSpec((tm, d), lambda i: (i, 0)),
            pl.BlockSpec((HALO, d), lambda i: (jnp.maximum(i * hb - 1, 0), 0)),
            pl.BlockSpec((HALO, d), lambda i: (jnp.minimum((i + 1) * hb, nhb - 1), 0)),
            pl.BlockSpec((None, N_MOD, d), lambda i: (geo.mod_row(i, tm), 0, 0)),
            pl.BlockSpec((4, d), lambda i: (0, 0)),
            pl.BlockSpec((CONV_W, d), lambda i: (0, 0)),
            pl.BlockSpec((1, d), lambda i: (0, 0)),
            pl.BlockSpec((1, d), lambda i: (0, 0)),
            pl.BlockSpec((1, d), lambda i: (0, 0)),
            pl.BlockSpec((None, d, d), lambda i: (lyr, 0, 0)),
            pl.BlockSpec((1, d), lambda i: (0, 0)),
        ],
        out_specs=pl.BlockSpec((tm, d), lambda i: (i, 0)),
        out_shape=jax.ShapeDtypeStruct((m, d), F32),
        scratch_shapes=[pltpu.VMEM((tm + 2 * HALO, d), F32), pltpu.VMEM((tm, d), F32),
                        pltpu.VMEM((SUBLANES - 1, tm + 2 * HALO, CONV_CC), F32)],
        compiler_params=_cparams(("arbitrary",)),
        name="conv_out",
    )(*xs, u, u, u, mod, ng, wdw, vec(bdw), vec(lng), vec(lnb), w2, vec(b2))


def _ffn_kernel(x_ref, mod_ref, ng_ref, wg_ref, wu_ref, wd_ref, o_ref, h_scr, acc):
    f = pl.program_id(1)

    @pl.when(f == 0)
    def _():
        h = _normmod(x_ref[...], ng_ref[NG_PRE_FFN:NG_PRE_FFN + 1, :],
                     mod_ref[SC2:SC2 + 1, :], mod_ref[SH2:SH2 + 1, :])
        h_scr[...] = h.astype(BF16)
        acc[...] = jnp.zeros_like(acc)

    h = h_scr[...]
    t = _silu(_dot(h, wg_ref[...])) * _dot(h, wu_ref[...])
    acc[...] += _dot(t.astype(BF16), wd_ref[...])

    @pl.when(f == pl.num_programs(1) - 1)
    def _():
        o_ref[...] = x_ref[...] + mod_ref[G2:G2 + 1, :] * (
            _rms(acc[...]) * ng_ref[NG_POST_FFN:NG_POST_FFN + 1, :])


def _ffn_call(geo, x, mod, ng, wg, wu, wd, lyr):
    m, d = x.shape
    dff = wg.shape[2]
    tm, tf = 512, 512
    geo.check_tile(tm)
    return pl.pallas_call(
        _ffn_kernel,
        grid=(m // tm, dff // tf),
        in_specs=[
            pl.BlockSpec((tm, d), lambda i, f: (i, 0)),
            pl.BlockSpec((None, N_MOD, d), lambda i, f: (geo.mod_row(i, tm), 0, 0)),
            pl.BlockSpec((4, d), lambda i, f: (0, 0)),
            pl.BlockSpec((None, d, tf), lambda i, f: (lyr, 0, f)),
            pl.BlockSpec((None, d, tf), lambda i, f: (lyr, 0, f)),
            pl.BlockSpec((None, tf, d), lambda i, f: (lyr, f, 0)),
        ],
        out_specs=pl.BlockSpec((tm, d), lambda i, f: (i, 0)),
        out_shape=jax.ShapeDtypeStruct((m, d), F32),
        scratch_shapes=[pltpu.VMEM((tm, d), BF16), pltpu.VMEM((tm, d), F32)],
        compiler_params=_cparams(("arbitrary", "arbitrary")),
        name="ffn",
    )(x, mod, ng, wg, wu, wd)


def _qkv_kernel(hd, scale, x_ref, mod_ref, ng_ref, w_ref, gq_ref, gk_ref, cos_ref, sin_ref,
                q_ref, k_ref, v_ref):
    nq = q_ref.shape[1]
    nk = k_ref.shape[1]
    for rows in _half_tiles(x_ref.shape[0]):
        h = _normmod(x_ref[rows, :], ng_ref[NG_PRE_MIX:NG_PRE_MIX + 1, :],
                     mod_ref[SC1:SC1 + 1, :], mod_ref[SH1:SH1 + 1, :])
        r = _dot(h.astype(BF16), w_ref[...])
        cos = cos_ref[rows, :]
        sin = sin_ref[rows, :]
        for c in range(0, nq, hd):
            qh = _rope(_rms(r[:, c:c + hd]) * gq_ref[...], cos, sin, hd // 4)
            q_ref[rows, c:c + hd] = (qh * scale).astype(BF16)
        for c in range(0, nk, hd):
            k_ref[rows, c:c + hd] = _rope(_rms(r[:, nq + c:nq + c + hd]) * gk_ref[...], cos, sin, hd // 4)
        v_ref[rows, :] = r[:, nq + nk:]


def _qkv_call(geo, x, mod, ng, wqkv, gq, gk, cos, sin):
    m, d = x.shape
    hd = gq.shape[-1]
    assert hd == LANES
    nq, nk = N_HEADS * hd, N_KV_HEADS * hd
    tm = 512
    geo.check_tile(tm)
    kern = functools.partial(_qkv_kernel, hd, hd ** -0.5)
    return pl.pallas_call(
        kern,
        grid=(m // tm,),
        in_specs=[
            pl.BlockSpec((tm, d), lambda i: (i, 0)),
            pl.BlockSpec((None, N_MOD, d), lambda i: (geo.mod_row(i, tm), 0, 0)),
            pl.BlockSpec((4, d), lambda i: (0, 0)),
            pl.BlockSpec((d, nq + 2 * nk), lambda i: (0, 0)),
            pl.BlockSpec((1, hd), lambda i: (0, 0)),
            pl.BlockSpec((1, hd), lambda i: (0, 0)),
            pl.BlockSpec((tm, LANES), lambda i: (geo.rope_blk(i, tm), 0)),
            pl.BlockSpec((tm, LANES), lambda i: (geo.rope_blk(i, tm), 0)),
        ],
        out_specs=[
            pl.BlockSpec((tm, nq), lambda i: (i, 0)),
            pl.BlockSpec((tm, nk), lambda i: (i, 0)),
            pl.BlockSpec((tm, nk), lambda i: (i, 0)),
        ],
        out_shape=[
            jax.ShapeDtypeStruct((m, nq), BF16),
            jax.ShapeDtypeStruct((m, nk), F32),
            jax.ShapeDtypeStruct((m, nk), F32),
        ],
        compiler_params=_cparams(("arbitrary",)),
        name="qkv",
    )(x, mod, ng, wqkv, gq.reshape(1, hd), gk.reshape(1, hd), cos, sin)


def _attn_kernel(n_group, dk, dv, shared_kv, n_kv_in, *refs):
    q_ref, kv_refs, o_ref, scr = refs[0], refs[1:1 + n_kv_in], refs[1 + n_kv_in], list(refs[2 + n_kv_in:])
    kv = []
    for r in kv_refs:
        if r.dtype == BF16:
            kv.append(r)
        else:
            s = scr.pop(0)
            pl.when(pl.program_id(2) == 0)(functools.partial(_store_bf16, s, r))
            kv.append(s)
    has_cache = n_kv_in == 4

    nt = (((1,), (1,)), ((), ()))
    for g in range(n_group):
        kc = slice(0, dk) if shared_kv else slice(g * dk, (g + 1) * dk)
        vc = slice(0, dv) if shared_kv else slice(g * dv, (g + 1) * dv)
        q = q_ref[:, g * dk:(g + 1) * dk]
        s_n = lax.dot_general(q, kv[0][:, kc], nt, preferred_element_type=F32)
        mx = jnp.max(s_n, axis=-1, keepdims=True)
        if has_cache:
            s_c = lax.dot_general(q, kv[2][:, kc], nt, preferred_element_type=F32)
            mx = jnp.maximum(mx, jnp.max(s_c, axis=-1, keepdims=True))
        p_n = jnp.exp(s_n - mx)
        den = jnp.sum(p_n, axis=-1, keepdims=True)
        o = _dot(p_n.astype(BF16), kv[1][:, vc])
        if has_cache:
            p_c = jnp.exp(s_c - mx)
            den = den + jnp.sum(p_c, axis=-1, keepdims=True)
            o = o + _dot(p_c.astype(BF16), kv[3][:, vc])
        o_ref[:, g * dv:(g + 1) * dv] = (o / den).astype(BF16)


def _store_bf16(dst, src):
    dst[...] = src[...].astype(BF16)


def _attn_call(q, kn, vn, kc, vc, *, n_batch, seq, row0, n_kv, n_group, dk, dv, tq, shared_kv, name):
    has_cache = kc is not None
    assert seq % tq == 0 and row0 % seq == 0
    nqt = seq // tq
    qb0, kb0 = row0 // tq, row0 // seq
    kw, vw = (dk, dv) if shared_kv else (n_group * dk, n_group * dv)
    in_specs = [
        pl.BlockSpec((tq, n_group * dk), lambda b, h, t: (qb0 + b * nqt + t, h)),
        pl.BlockSpec((seq, kw), lambda b, h, t: (kb0 + b, h)),
        pl.BlockSpec((seq, vw), lambda b, h, t: (kb0 + b, h)),
    ]
    args = [q, kn, vn]
    rows = [seq, seq]
    if has_cache:
        past = kc.shape[1]
        in_specs += [
            pl.BlockSpec((None, past, kw), lambda b, h, t: (b, 0, h)),
            pl.BlockSpec((None, past, vw), lambda b, h, t: (b, 0, h)),
        ]
        args += [kc, vc]
        rows += [past, past]
    scratch = [pltpu.VMEM((r, w), BF16)
               for a, r, w in zip(args[1:], rows, (kw, vw, kw, vw)) if a.dtype != BF16]
    kern = functools.partial(_attn_kernel, n_group, dk, dv, shared_kv, len(args) - 1)
    return pl.pallas_call(
        kern,
        grid=(n_batch, n_kv, nqt),
        in_specs=in_specs,
        out_specs=pl.BlockSpec((tq, n_group * dv), lambda b, h, t: (b * nqt + t, h)),
        out_shape=jax.ShapeDtypeStruct((n_batch * seq, n_kv * n_group * dv), BF16),
        scratch_shapes=scratch,
        compiler_params=_cparams(("arbitrary", "arbitrary", "arbitrary")),
        name=name,
    )(*args)


def _attn_out_kernel(npt, x_ref, op_ref, os_ref, mod_ref, ng_ref, wo_ref, y_ref):
    def project(o_ref):
        out = _dot(o_ref[...], wo_ref[...])
        y_ref[...] = x_ref[...] + mod_ref[G1:G1 + 1, :] * (
            _rms(out) * ng_ref[NG_POST_MIX:NG_POST_MIX + 1, :])

    i = pl.program_id(0)
    pl.when(i < npt)(lambda: project(op_ref))
    pl.when(i >= npt)(lambda: project(os_ref))


def _attn_out_call(geo, x, o_p, o_s, mod, ng, wo):
    m, d = x.shape
    tm = 512
    geo.check_tile(tm)
    npt = geo.P // tm
    n = o_p.shape[1]
    return pl.pallas_call(
        functools.partial(_attn_out_kernel, npt),
        grid=(m // tm,),
        in_specs=[
            pl.BlockSpec((tm, d), lambda i: (i, 0)),
            pl.BlockSpec((tm, n), lambda i: (jnp.minimum(i, npt - 1), 0)),
            pl.BlockSpec((tm, n), lambda i: (jnp.maximum(i - npt, 0), 0)),
            pl.BlockSpec((None, N_MOD, d), lambda i: (geo.mod_row(i, tm), 0, 0)),
            pl.BlockSpec((4, d), lambda i: (0, 0)),
            pl.BlockSpec(wo.shape, lambda i: (0, 0)),
        ],
        out_specs=pl.BlockSpec((tm, d), lambda i: (i, 0)),
        out_shape=jax.ShapeDtypeStruct((m, d), F32),
        compiler_params=_cparams(("arbitrary",)),
        name="attn_out",
    )(x, o_p, o_s, mod, ng, wo)


HI_MASK = 0xFFFF0000


def _as_bf16_bits(v):
    return lax.bitcast_convert_type(v.astype(BF16).astype(F32), jnp.uint32)


def _pack_rows(v, out_ref):
    n, d = v.shape
    assert d == 2 * SUBLANES * LANES
    for s in range(SUBLANES):
        lo = _as_bf16_bits(v[:, s * LANES:(s + 1) * LANES])
        hi = _as_bf16_bits(v[:, (s + SUBLANES) * LANES:(s + SUBLANES + 1) * LANES])
        out_ref[pl.ds(s, n, stride=SUBLANES), :] = (lo >> 16) | (hi & jnp.uint32(HI_MASK))


def _unpack_rows(buf, n, s):
    w = buf[pl.ds(s, n, stride=SUBLANES), :]
    lo = lax.bitcast_convert_type(w << 16, F32)
    hi = lax.bitcast_convert_type(w & jnp.uint32(HI_MASK), F32)
    return lo, hi


def _tile_copy(src_hbm, src_row, dst, dst_row, sem):
    return pltpu.make_async_copy(
        src_hbm.at[pl.ds(pl.multiple_of(src_row * SUBLANES, SUBLANES), SUBLANES), :],
        dst.at[pl.ds(pl.multiple_of(dst_row * SUBLANES, SUBLANES), SUBLANES), :], sem)


def _start_gather(idx_ref, src_hbm, dst, n, sem):
    def start(r2, c):
        for pr in (0, 1):
            r = 2 * r2 + pr
            _tile_copy(src_hbm, idx_ref[0, r], dst, r, sem).start(priority=pr)
        return c

    assert n % 2 == 0
    lax.fori_loop(0, n // 2, start, 0, unroll=4)


def _wait_gather(src_hbm, dst, n, sem):
    pltpu.make_async_copy(src_hbm.at[pl.ds(0, n * SUBLANES), :], dst, sem).wait()


def _pipelined_gather(i, n_steps, idx_ref, nxt_ref, src_hbm, bufs, n, sems, consume):
    for sl in (0, 1):
        @pl.when(i % 2 == sl)
        def _(sl=sl):
            if sl == 0:
                @pl.when(i == 0)
                def _():
                    _start_gather(idx_ref, src_hbm, bufs[0], n, sems.at[0])

            @pl.when(i + 1 < n_steps)
            def _():
                _start_gather(nxt_ref, src_hbm, bufs[1 - sl], n, sems.at[1 - sl])

            _wait_gather(src_hbm, bufs[sl], n, sems.at[sl])
            consume(bufs[sl])


def _route_kernel(n_exp, x_ref, mod_ref, ng_ref, wr_ref, br_ref, h_ref, topw_ref, topi_ref):
    h = _normmod(x_ref[...], ng_ref[NG_PRE_FFN:NG_PRE_FFN + 1, :],
                 mod_ref[SC2:SC2 + 1, :], mod_ref[SH2:SH2 + 1, :])
    _pack_rows(h, h_ref)
    logits = jnp.dot(h, wr_ref[...], preferred_element_type=F32,
                     precision=lax.Precision.HIGHEST) + br_ref[...]
    lane = lax.broadcasted_iota(jnp.int32, logits.shape, 1)
    neg = jnp.float32(-jnp.inf)
    logits = jnp.where(lane < n_exp, logits, neg)
    m1 = jnp.max(logits, axis=-1, keepdims=True)
    i1 = jnp.min(jnp.where(logits == m1, lane, LANES), axis=-1, keepdims=True)
    rest = jnp.where(lane == i1, neg, logits)
    m2 = jnp.max(rest, axis=-1, keepdims=True)
    i2 = jnp.min(jnp.where(rest == m2, lane, LANES), axis=-1, keepdims=True)
    e = jnp.exp(m2 - m1)
    w1 = 1.0 / (1.0 + e)
    w2 = e / (1.0 + e)
    topw_ref[...] = jnp.where(lane == 0, w1, jnp.where(lane == 1, w2, 0.0))
    topi_ref[...] = jnp.where(lane == 0, i1, jnp.where(lane == 1, i2, 0))


def _route_call(geo, x, mod, ng, wr_pad, br_pad):
    m, d = x.shape
    tm = 512
    geo.check_tile(tm)
    kern = functools.partial(_route_kernel, N_EXPERTS)
    return pl.pallas_call(
        kern,
        grid=(m // tm,),
        in_specs=[
            pl.BlockSpec((tm, d), lambda i: (i, 0)),
            pl.BlockSpec((None, N_MOD, d), lambda i: (geo.mod_row(i, tm), 0, 0)),
            pl.BlockSpec((4, d), lambda i: (0, 0)),
            pl.BlockSpec((d, LANES), lambda i: (0, 0)),
            pl.BlockSpec((1, LANES), lambda i: (0, 0)),
        ],
        out_specs=[
            pl.BlockSpec((tm * SUBLANES, LANES), lambda i: (i, 0)),
            pl.BlockSpec((tm, LANES), lambda i: (i, 0)),
            pl.BlockSpec((tm, LANES), lambda i: (i, 0)),
        ],
        out_shape=[
            jax.ShapeDtypeStruct((m * SUBLANES, LANES), jnp.uint32),
            jax.ShapeDtypeStruct((m, LANES), F32),
            jax.ShapeDtypeStruct((m, LANES), jnp.int32),
        ],
        compiler_params=_cparams(("arbitrary",)),
        name="route",
    )(x, mod, ng, wr_pad, br_pad)


def _dispatch(topi, topw, tr):
    m = topi.shape[0]
    n_asg = TOP_K * m
    n_tiles = n_asg // tr + N_EXPERTS
    e = topi[:, :TOP_K].reshape(n_asg)
    w = topw[:, :TOP_K].reshape(n_asg)
    onehot = (e[:, None] == jnp.arange(N_EXPERTS, dtype=jnp.int32)[None, :]).astype(jnp.int32)
    csum = jnp.cumsum(onehot, axis=0)
    count = csum[-1]
    rank = jnp.sum(csum * onehot, axis=1) - 1
    ptiles = (count + tr - 1) // tr
    tile_end = jnp.cumsum(ptiles)
    tile_start = tile_end - ptiles
    n_used = tile_end[-1:]
    dest = tile_start[e] * tr + rank
    asg = jnp.full((n_tiles * tr,), -1, jnp.int32).at[dest].set(jnp.arange(n_asg, dtype=jnp.int32))
    src = jnp.maximum(asg, 0) // TOP_K
    gate = jnp.where(asg >= 0, w[jnp.maximum(asg, 0)], 0.0)
    tile_expert = jnp.minimum(
        jnp.sum((jnp.arange(n_tiles, dtype=jnp.int32)[:, None] >= tile_end[None, :]).astype(jnp.int32), axis=1),
        N_EXPERTS - 1)
    return src, gate, tile_expert, n_used.astype(jnp.int32), dest


def _moe_kernel(te_ref, nu_ref, src_ref, nxt_ref, h_hbm, g_ref, wg_ref, wu_ref, wd_ref, y_ref,
                buf0, buf1, hbf, acc, sems):
    i = pl.program_id(0)
    f = pl.program_id(1)
    n_used = nu_ref[0]
    used = i < n_used
    tr = hbf.shape[0]

    def unpack(buf):
        for s in range(SUBLANES):
            lo, hi = _unpack_rows(buf, tr, s)
            hbf[:, s * LANES:(s + 1) * LANES] = lo.astype(BF16)
            hbf[:, (s + SUBLANES) * LANES:(s + SUBLANES + 1) * LANES] = hi.astype(BF16)

    @pl.when(used & (f == 0))
    def _():
        _pipelined_gather(i, n_used, src_ref, nxt_ref, h_hbm, (buf0, buf1), tr, sems, unpack)
        acc[...] = jnp.zeros_like(acc)

    @pl.when(used)
    def _():
        h = hbf[...]
        g = g_ref[...]
        tf = wg_ref.shape[1]
        for c0 in range(0, tf, MOE_TC):
            c1 = min(c0 + MOE_TC, tf)
            t = _silu(_dot(h, wg_ref[:, c0:c1])) * _dot(h, wu_ref[:, c0:c1]) * g
            acc[...] += _dot(t.astype(BF16), wd_ref[c0:c1, :])

    last = f == pl.num_programs(1) - 1

    @pl.when(used & last)
    def _():
        _pack_rows(acc[...], y_ref)

    @pl.when(jnp.logical_not(used) & last)
    def _():
        y_ref[...] = jnp.zeros_like(y_ref)


def _moe_call(h, src, gate, tile_expert, n_used, wg, wu, wd, lyr, tr):
    _, n_exp, d, dfe = wg.shape
    n_tiles = tile_expert.shape[0]
    tf = MOE_TF
    assert dfe % tf == 0
    nf = dfe // tf

    def fblk(i, f, nu):
        return jnp.where(i < nu[0], f, nf - 1)

    src = src.reshape(n_tiles, 1, tr)
    tile_rows = tr * SUBLANES
    grid_spec = pltpu.PrefetchScalarGridSpec(
        num_scalar_prefetch=2,
        grid=(n_tiles, nf),
        in_specs=[
            pl.BlockSpec((None, 1, tr), lambda i, f, te, nu: (i, 0, 0), memory_space=pltpu.SMEM),
            pl.BlockSpec((None, 1, tr), lambda i, f, te, nu: (jnp.minimum(i + 1, n_tiles - 1), 0, 0),
                         memory_space=pltpu.SMEM),
            pl.BlockSpec(memory_space=pl.ANY),
            pl.BlockSpec((tr, 1), lambda i, f, te, nu: (i, 0)),
            pl.BlockSpec((None, None, d, tf), lambda i, f, te, nu: (lyr, te[i], 0, fblk(i, f, nu))),
            pl.BlockSpec((None, None, d, tf), lambda i, f, te, nu: (lyr, te[i], 0, fblk(i, f, nu))),
            pl.BlockSpec((None, None, tf, d), lambda i, f, te, nu: (lyr, te[i], fblk(i, f, nu), 0)),
        ],
        out_specs=pl.BlockSpec((tile_rows, LANES), lambda i, f, te, nu: (i, 0)),
        scratch_shapes=[pltpu.VMEM((tile_rows, LANES), jnp.uint32), pltpu.VMEM((tile_rows, LANES), jnp.uint32),
                        pltpu.VMEM((tr, d), BF16), pltpu.VMEM((tr, d), F32),
                        pltpu.SemaphoreType.DMA((2,))],
    )
    return pl.pallas_call(
        _moe_kernel,
        grid_spec=grid_spec,
        out_shape=jax.ShapeDtypeStruct((n_tiles * tile_rows, LANES), jnp.uint32),
        compiler_params=_cparams(("arbitrary", "arbitrary")),
        name="moe",
    )(tile_expert, n_used, src, src, h, gate.reshape(n_tiles * tr, 1), wg, wu, wd)


def _combine_kernel(npt, pos_ref, nxt_ref, y_hbm, x_ref, mod_ref, ng_ref, *refs):
    outs, (buf0, buf1, ysum, sems) = refs[:-4], refs[-4:]
    i = pl.program_id(0)
    tm = x_ref.shape[0]

    def add_pairs(buf):
        for s in range(SUBLANES):
            lo, hi = _unpack_rows(buf, TOP_K * tm, s)
            ysum[:, s * LANES:(s + 1) * LANES] = lo[:tm] + lo[tm:]
            ysum[:, (s + SUBLANES) * LANES:(s + SUBLANES + 1) * LANES] = hi[:tm] + hi[tm:]

    _pipelined_gather(i, pl.num_programs(0), pos_ref, nxt_ref, y_hbm, (buf0, buf1), TOP_K * tm, sems,
                      add_pairs)
    res = x_ref[...] + mod_ref[G2:G2 + 1, :] * (
        _rms(ysum[...]) * ng_ref[NG_POST_FFN:NG_POST_FFN + 1, :])
    if npt is None:
        outs[0][...] = res
    else:
        @pl.when(i < npt)
        def _():
            outs[0][...] = res

        @pl.when(i >= npt)
        def _():
            outs[1][...] = res


def _combine_call(geo, x, y, dest, mod, ng, split):
    m, d = x.shape
    tm = 256
    geo.check_tile(tm)
    nt = m // tm
    npt = geo.P // tm
    pos = dest.reshape(nt, tm, TOP_K).transpose(0, 2, 1).reshape(nt, 1, TOP_K * tm)
    if split:
        out_specs = [pl.BlockSpec((tm, d), lambda i: (jnp.minimum(i, npt - 1), 0)),
                     pl.BlockSpec((tm, d), lambda i: (jnp.maximum(i - npt, 0), 0))]
        out_shape = [jax.ShapeDtypeStruct((geo.P, d), F32), jax.ShapeDtypeStruct((m - geo.P, d), F32)]
    else:
        out_specs = pl.BlockSpec((tm, d), lambda i: (i, 0))
        out_shape = jax.ShapeDtypeStruct((m, d), F32)
    buf = pltpu.VMEM((TOP_K * tm * SUBLANES, LANES), jnp.uint32)
    return pl.pallas_call(
        functools.partial(_combine_kernel, npt if split else None),
        grid=(nt,),
        in_specs=[
            pl.BlockSpec((None, 1, TOP_K * tm), lambda i: (i, 0, 0), memory_space=pltpu.SMEM),
            pl.BlockSpec((None, 1, TOP_K * tm), lambda i: (jnp.minimum(i + 1, nt - 1), 0, 0),
                         memory_space=pltpu.SMEM),
            pl.BlockSpec(memory_space=pl.ANY),
            pl.BlockSpec((tm, d), lambda i: (i, 0)),
            pl.BlockSpec((None, N_MOD, d), lambda i: (geo.mod_row(i, tm), 0, 0)),
            pl.BlockSpec((4, d), lambda i: (0, 0)),
        ],
        out_specs=out_specs,
        out_shape=out_shape,
        scratch_shapes=[buf, buf, pltpu.VMEM((tm, d), F32), pltpu.SemaphoreType.DMA((2,))],
        compiler_params=_cparams(("arbitrary",)),
        name="moe_combine",
    )(pos, pos, y, x, mod, ng)


def _mla_in_kernel(scale, x_ref, mod_ref, ng_ref, wd_ref, gq_ref, gkv_ref, wuq_ref, wuk_ref, wuv_ref,
                   cos_ref, sin_ref, q_ref, kcat_ref, v_ref, ckv_ref, kpe_ref):
    qr = gq_ref.shape[1]
    kr = gkv_ref.shape[1]
    rope_dim = kpe_ref.shape[1]
    h = _normmod(x_ref[...], ng_ref[NG_PRE_MIX:NG_PRE_MIX + 1, :],
                 mod_ref[SC1:SC1 + 1, :], mod_ref[SH1:SH1 + 1, :])
    r = _dot(h.astype(BF16), wd_ref[...])
    cos = cos_ref[...]
    sin = sin_ref[...]
    cq = (_rms(r[:, :qr]) * gq_ref[...]).astype(BF16)
    ckv = _rms(r[:, qr:qr + kr]) * gkv_ref[...]
    kpe = r[:, qr + kr:qr + kr + LANES]
    ckv_ref[...] = ckv
    kpe_ref[...] = kpe[:, :rope_dim]
    ckv_b = ckv.astype(BF16)
    q = _dot(cq, wuq_ref[...])
    kn = _dot(ckv_b, wuk_ref[...])
    v_ref[...] = _dot(ckv_b, wuv_ref[...]).astype(BF16)
    kpe_r = _rope(kpe, cos, sin, rope_dim // 4).astype(BF16)
    for hh in range(q_ref.shape[1] // MLA_QK):
        c = hh * MLA_QK
        q_ref[:, c:c + NOPE_DIM] = (q[:, c:c + NOPE_DIM] * scale).astype(BF16)
        q_ref[:, c + NOPE_DIM:c + MLA_QK] = (
            _rope(q[:, c + NOPE_DIM:c + MLA_QK], cos, sin, rope_dim // 4) * scale).astype(BF16)
        kcat_ref[:, c:c + NOPE_DIM] = kn[:, hh * NOPE_DIM:(hh + 1) * NOPE_DIM].astype(BF16)
        kcat_ref[:, c + NOPE_DIM:c + MLA_QK] = kpe_r


def _mla_in_call(geo, x, mod, ng, wd_cat, gq, gkv, wuq_pad, wuk, wuv, cos, sin):
    m, d = x.shape
    qr, kr = gq.shape[-1], gkv.shape[-1]
    nqk = wuq_pad.shape[1]
    nv = wuv.shape[1]
    tm = 512
    geo.check_tile(tm)
    kern = functools.partial(_mla_in_kernel, (NOPE_DIM + ROPE_DIM) ** -0.5)
    full = lambda a: pl.BlockSpec(a.shape, lambda i: (0, 0))
    row = lambda n: pl.BlockSpec((tm, n), lambda i: (i, 0))
    return pl.pallas_call(
        kern,
        grid=(m // tm,),
        in_specs=[
            row(d),
            pl.BlockSpec((None, N_MOD, d), lambda i: (geo.mod_row(i, tm), 0, 0)),
            pl.BlockSpec((4, d), lambda i: (0, 0)),
            full(wd_cat),
            pl.BlockSpec((1, qr), lambda i: (0, 0)),
            pl.BlockSpec((1, kr), lambda i: (0, 0)),
            full(wuq_pad), full(wuk), full(wuv),
            pl.BlockSpec((tm, LANES), lambda i: (geo.rope_blk(i, tm), 0)),
            pl.BlockSpec((tm, LANES), lambda i: (geo.rope_blk(i, tm), 0)),
        ],
        out_specs=[row(nqk), row(nqk), row(nv), row(kr), row(ROPE_DIM)],
        out_shape=[
            jax.ShapeDtypeStruct((m, nqk), BF16),
            jax.ShapeDtypeStruct((m, nqk), BF16),
            jax.ShapeDtypeStruct((m, nv), BF16),
            jax.ShapeDtypeStruct((m, kr), F32),
            jax.ShapeDtypeStruct((m, ROPE_DIM), F32),
        ],
        compiler_params=_cparams(("arbitrary",)),
        name="mla_in",
    )(x, mod, ng, wd_cat, gq.reshape(1, qr), gkv.reshape(1, kr), wuq_pad, wuk, wuv, cos, sin)


def _mla_cache_kernel(ckv_ref, kpe_ref, wuk_ref, wuv_ref, kcat_ref, v_ref):
    ckv = ckv_ref[...].astype(BF16)
    kn = _dot(ckv, wuk_ref[...])
    v_ref[...] = _dot(ckv, wuv_ref[...]).astype(BF16)
    kpe = kpe_ref[...].astype(BF16)
    for hh in range(kcat_ref.shape[1] // MLA_QK):
        c = hh * MLA_QK
        kcat_ref[:, c:c + NOPE_DIM] = kn[:, hh * NOPE_DIM:(hh + 1) * NOPE_DIM].astype(BF16)
        kcat_ref[:, c + NOPE_DIM:c + MLA_QK] = kpe


def _mla_cache_call(ckv, kpe_pad, wuk, wuv):
    n, kr = ckv.shape
    tm = 512
    assert n % tm == 0
    nqk = MLA_HEADS * MLA_QK
    nv = wuv.shape[1]
    return pl.pallas_call(
        _mla_cache_kernel,
        grid=(n // tm,),
        in_specs=[
            pl.BlockSpec((tm, kr), lambda i: (i, 0)),
            pl.BlockSpec((tm, LANES), lambda i: (i, 0)),
            pl.BlockSpec(wuk.shape, lambda i: (0, 0)),
            pl.BlockSpec(wuv.shape, lambda i: (0, 0)),
        ],
        out_specs=[pl.BlockSpec((tm, nqk), lambda i: (i, 0)), pl.BlockSpec((tm, nv), lambda i: (i, 0))],
        out_shape=[jax.ShapeDtypeStruct((n, nqk), BF16), jax.ShapeDtypeStruct((n, nv), BF16)],
        compiler_params=_cparams(("arbitrary",)),
        name="mla_cache",
    )(ckv, kpe_pad, wuk, wuv)


def _rope_table(n_tokens, dim, n_ident):
    rows = n_tokens // GRID_W
    row = jnp.repeat(jnp.arange(rows, dtype=F32), GRID_W)
    col = jnp.tile(jnp.arange(GRID_W, dtype=F32), rows)
    half = dim // 2
    inv = ROPE_THETA ** (-jnp.arange(0, half, 2, dtype=F32) / half)
    ar = row[:, None] * inv[None, :]
    ac = col[:, None] * inv[None, :]
    ang = jnp.concatenate([ar, ar, ac, ac], axis=-1)
    cos = jnp.cos(ang)
    sin = jnp.sin(ang)
    hb = dim // 4
    first = (jnp.arange(dim) % (2 * hb)) < hb
    sin = jnp.where(first[None, :], -sin, sin)
    cos = jnp.pad(cos, ((0, n_ident), (0, LANES - dim)), constant_values=1.0)
    sin = jnp.pad(sin, ((0, n_ident), (0, LANES - dim)))
    return cos, sin


def kernel(x_prompt, x_sample, cache_gqa_k, cache_gqa_v, cache_mla_ckv, cache_mla_kpe, c, c_ctx,
           w_mod, b_mod, norm_g,
           conv_w1, conv_b1, conv_wdw, conv_bdw, conv_ln_g, conv_ln_b, conv_w2, conv_b2,
           gqa_wq, gqa_wk, gqa_wv, gqa_gq, gqa_gk, gqa_wo,
           mla_wdq, mla_gq, mla_wuq, mla_wdkv, mla_gkv, mla_wuk, mla_wuv, mla_wo,
           ffn_wg, ffn_wu, ffn_wd,
           moe_wr, moe_br, moe_wg, moe_wu, moe_wd):
    b, s, d = x_prompt.shape
    db, t, _ = x_sample.shape
    geo = _Geom(b, s, db, t)
    depth = w_mod.shape[0]
    past = cache_gqa_k.shape[2]
    hd = gqa_gq.shape[-1]
    bf = lambda a: a.astype(BF16)

    assert N_MIXERS == 3
    x = (x_prompt.reshape(geo.P, d), x_sample.reshape(db * t, d))

    n_cond = 1 + db
    cond = jnp.concatenate([c_ctx[None, :], c], axis=0)
    cond = jnp.pad(cond, ((0, -n_cond % 8), (0, 0)))
    mods = _mod_call(cond, w_mod, b_mod).reshape(depth, cond.shape[0], N_MOD, d)

    rope_tm = 512
    cos_hd, sin_hd = _rope_table(t, hd, rope_tm)
    cos_pe, sin_pe = _rope_table(t, ROPE_DIM, rope_tm)

    conv_w1_b, conv_w2_b = bf(conv_w1), bf(conv_w2)
    ffn_wg_b, ffn_wu_b, ffn_wd_b = bf(ffn_wg), bf(ffn_wu), bf(ffn_wd)
    moe_wg_b, moe_wu_b, moe_wd_b = bf(moe_wg), bf(moe_wu), bf(moe_wd)

    ks, vs, cs, ps = [], [], [], []
    for i in range(depth):
        mod, ng = mods[i], norm_g[i]
        j = i // N_MIXERS
        if i % N_MIXERS == 0:
            u = _conv_in_call(geo, x, mod, ng, conv_w1_b, j, conv_b1[j])
            x = _conv_out_call(geo, u, x, mod, ng, conv_wdw[j], conv_bdw[j], conv_ln_g[j],
                               conv_ln_b[j], conv_w2_b, j, conv_b2[j])
        elif i % N_MIXERS == 1:
            wqkv = bf(jnp.concatenate([gqa_wq[j], gqa_wk[j], gqa_wv[j]], axis=1))
            q, k, v = _qkv_call(geo, x, mod, ng, wqkv, gqa_gq[j], gqa_gk[j], cos_hd, sin_hd)
            ks.append(k[:geo.P].reshape(b, s, N_KV_HEADS, hd))
            vs.append(v[:geo.P].reshape(b, s, N_KV_HEADS, hd))
            kw = dict(n_kv=N_KV_HEADS, n_group=N_HEADS // N_KV_HEADS, dk=hd, dv=hd, shared_kv=True)
            o_p = _attn_call(q, k, v, None, None, n_batch=b, seq=s, row0=0, tq=min(s, 256),
                             name="gqa_attn_prompt", **kw)
            o_s = _attn_call(q, k, v,
                             cache_gqa_k[:, j].reshape(db, past, N_KV_HEADS * hd),
                             cache_gqa_v[:, j].reshape(db, past, N_KV_HEADS * hd),
                             n_batch=db, seq=t, row0=geo.P, tq=256, name="gqa_attn_sample", **kw)
            x = _attn_out_call(geo, x, o_p, o_s, mod, ng, bf(gqa_wo[j]))
        else:
            qr, kr = mla_gq.shape[-1], mla_gkv.shape[-1]
            wd_cat = bf(jnp.pad(jnp.concatenate([mla_wdq[j], mla_wdkv[j]], axis=1),
                                ((0, 0), (0, LANES - ROPE_DIM))))
            wuq_pad = bf(jnp.pad(mla_wuq[j], ((0, 0), (0, 0), (0, MLA_QK - NOPE_DIM - ROPE_DIM)))
                         ).reshape(qr, MLA_HEADS * MLA_QK)
            wuk = bf(mla_wuk[j]).reshape(kr, MLA_HEADS * NOPE_DIM)
            wuv = bf(mla_wuv[j]).reshape(kr, MLA_HEADS * V_DIM)
            q, kcat, v, ckv, kpe = _mla_in_call(geo, x, mod, ng, wd_cat, mla_gq[j], mla_gkv[j],
                                                wuq_pad, wuk, wuv, cos_pe, sin_pe)
            cs.append(ckv[:geo.P].reshape(b, s, kr))
            ps.append(kpe[:geo.P].reshape(b, s, ROPE_DIM))
            kc, vc = _mla_cache_call(
                cache_mla_ckv[:, j].reshape(db * past, kr),
                jnp.pad(cache_mla_kpe[:, j].reshape(db * past, ROPE_DIM), ((0, 0), (0, LANES - ROPE_DIM))),
                wuk, wuv)
            kw = dict(n_kv=MLA_HEADS // MLA_HPS, n_group=MLA_HPS, dk=MLA_QK, dv=V_DIM, shared_kv=False)
            o_p = _attn_call(q, kcat, v, None, None, n_batch=b, seq=s, row0=0, tq=min(s, 256),
                             name="mla_attn_prompt", **kw)
            o_s = _attn_call(q, kcat, v, kc.reshape(db, past, -1), vc.reshape(db, past, -1),
                             n_batch=db, seq=t, row0=geo.P, tq=256, name="mla_attn_sample", **kw)
            x = _attn_out_call(geo, x, o_p, o_s, mod, ng, bf(mla_wo[j]))
        f = i // 2
        if i % 2 == 0:
            x = _ffn_call(geo, x, mod, ng, ffn_wg_b, ffn_wu_b, ffn_wd_b, f)
        else:
            wr_pad = jnp.pad(moe_wr[f], ((0, 0), (0, LANES - N_EXPERTS)))
            br_pad = jnp.pad(moe_br[f], (0, LANES - N_EXPERTS)).reshape(1, LANES)
            h, topw, topi = _route_call(geo, x, mod, ng, wr_pad, br_pad)
            src, gate, tile_expert, n_used, dest = _dispatch(topi, topw, MOE_TR)
            y = _moe_call(h, src, gate, tile_expert, n_used, moe_wg_b, moe_wu_b, moe_wd_b, f, MOE_TR)
            x = _combine_call(geo, x, y, dest, mod, ng, split=(i == depth - 1))

    y_prompt, y_sample = x if isinstance(x, (list, tuple)) else (x[:geo.P], x[geo.P:])
    return (y_prompt.reshape(b, s, d), y_sample.reshape(db, t, d), jnp.stack(ks, axis=1), jnp.stack(vs, axis=1),
            jnp.stack(cs, axis=1), jnp.stack(ps, axis=1))
```
